```python
import jax
import jax.numpy as jnp
from jax import lax
import numpy as np

D_MODEL = 1024
BATCH = 32
SEQ = 256
DEPTH = 4
DEC_BATCH = 4
DEC_SEQ = 4096
PAST_LEN = 512

GRID_W = 64
ROPE_THETA = 10000.0
EPS = 1e-6
NEG_INF = -1e30
Q_BLOCK = 128

SWA_HEADS = 8
SWA_KV_HEADS = 2
SWA_GROUP = SWA_HEADS // SWA_KV_HEADS
SWA_HEAD_DIM = 64
SWA_WINDOW = 128
SWA_Q_COLS = SWA_HEADS * SWA_HEAD_DIM
SWA_KV_COLS = SWA_KV_HEADS * SWA_HEAD_DIM

MLA_HEADS = 8
MLA_Q_LORA = 256
MLA_KV_LORA = 128
MLA_NOPE = 64
MLA_ROPE = 32
MLA_V = 64
MLA_QK = MLA_NOPE + MLA_ROPE

GM_CHUNK = 128
GM_GROUPS = 4
GM_WIDTH = 512
GM_GROUP_DIM = GM_WIDTH // GM_GROUPS

IN_SPLITS = (SWA_Q_COLS, SWA_KV_COLS, SWA_KV_COLS, MLA_Q_LORA, MLA_KV_LORA, MLA_ROPE, GM_WIDTH, GM_WIDTH, D_MODEL, D_MODEL, D_MODEL)
IN_COLS = SWA_Q_COLS + 2 * SWA_KV_COLS + MLA_Q_LORA + MLA_KV_LORA + MLA_ROPE + 2 * GM_WIDTH + 3 * D_MODEL

N_EXPERT_GROUPS = 4
EXPERTS_PER_GROUP = 4
N_EXPERTS = N_EXPERT_GROUPS * EXPERTS_PER_GROUP
MOE_TOP_K = 2
EXPERT_FF = 512

kernel_name = 'hybrid_diffusion_prefix_trunk_step'


def rms_norm(x, g):
    xf = x.astype(jnp.float32)
    y = xf * lax.rsqrt(jnp.mean(xf * xf, axis=-1, keepdims=True) + EPS)
    return (y * g.astype(jnp.float32)).astype(x.dtype)


def layer_norm(x, g, b):
    xf = x.astype(jnp.float32)
    xc = xf - jnp.mean(xf, axis=-1, keepdims=True)
    var = jnp.mean(xc * xc, axis=-1, keepdims=True)
    return (xc * lax.rsqrt(var + EPS) * g.astype(jnp.float32) + b.astype(jnp.float32)).astype(x.dtype)


def modulate(h, shift, scale):
    return h * (1 + scale[:, None, :]) + shift[:, None, :]


def adaln(cond, w_mod, b_mod):
    m = jax.nn.silu(cond) @ w_mod + b_mod
    return jnp.split(m, 6, axis=-1)


def split_columns(z):
    out, start = [], 0
    for width in IN_SPLITS:
        out.append(z[..., start:start + width])
        start += width
    return out


def axial_rope_tables(n_tokens, rot_dim):
    n_rows = n_tokens // GRID_W
    row = jnp.broadcast_to(jnp.arange(n_rows, dtype=jnp.float32)[:, None], (n_rows, GRID_W)).reshape(-1)
    col = jnp.broadcast_to(jnp.arange(GRID_W, dtype=jnp.float32)[None, :], (n_rows, GRID_W)).reshape(-1)
    q4 = rot_dim // 4
    inv_freq = ROPE_THETA ** (-jnp.arange(q4, dtype=jnp.float32) / q4)
    ang = jnp.stack([row[:, None] * inv_freq, col[:, None] * inv_freq], axis=1)
    return jnp.cos(ang), jnp.sin(ang)


def apply_axial_rope(x, cos, sin):
    n_tok, rot = x.shape[1], x.shape[-1]
    q4 = rot // 4
    bshape = (1, n_tok) + (1,) * (x.ndim - 3) + (2, q4)
    c, s = cos.reshape(bshape), sin.reshape(bshape)
    xr = x.astype(jnp.float32).reshape(x.shape[:-1] + (2, 2, q4))
    x1, x2 = xr[..., 0, :], xr[..., 1, :]
    out = jnp.stack([x1 * c - x2 * s, x2 * c + x1 * s], axis=-2)
    return out.reshape(x.shape).astype(x.dtype)


def attn_probs(s, sink):
    if sink is None:
        return jax.nn.softmax(s, axis=-1)
    sk = jnp.broadcast_to(sink.astype(jnp.float32)[None, :, :, None, None], s.shape[:-1] + (1,))
    return jax.nn.softmax(jnp.concatenate([s, sk], axis=-1), axis=-1)[..., :-1]


def blocked_attention(q, k, v, scale, sink=None):
    B, S, KH, G, Dq = q.shape
    nb = S // Q_BLOCK
    qb = jnp.moveaxis(q.reshape(B, nb, Q_BLOCK, KH, G, Dq), 1, 0)

    def one_block(qblk):
        s = jnp.einsum('bqkgd,bmkd->bkgqm', qblk, k, preferred_element_type=jnp.float32) * scale
        p = attn_probs(s, sink).astype(v.dtype)
        return jnp.einsum('bkgqm,bmkd->bqkgd', p, v, preferred_element_type=jnp.float32).astype(q.dtype)

    out = lax.map(one_block, qb)
    return jnp.moveaxis(out, 0, 1).reshape(B, S, KH * G * v.shape[-1])


def window_attention(q, k, v, k_ctx, v_ctx, sink, scale):
    B, S, KH, G, Dh = q.shape
    W = SWA_WINDOW
    nb = S // W
    L = k_ctx.shape[1]
    qb = jnp.moveaxis(q.reshape(B, nb, W, KH, G, Dh), 1, 0)

    def band(t):
        tp = jnp.pad(t, ((0, 0), (W, W), (0, 0), (0, 0))).reshape(B, nb + 2, W, KH, t.shape[-1])
        return jnp.moveaxis(jnp.concatenate([tp[:, :-2], tp[:, 1:-1], tp[:, 2:]], axis=2), 1, 0)

    qpos = jnp.arange(nb)[:, None] * W + jnp.arange(W)[None, :]
    kpos = (jnp.arange(nb)[:, None] - 1) * W + jnp.arange(3 * W)[None, :]
    rel = kpos[:, None, :] - qpos[:, :, None]
    valid = (jnp.abs(rel) <= W) & (kpos[:, None, :] >= 0) & (kpos[:, None, :] < S)

    def one_block(args):
        qblk, kblk, vblk, vmask = args
        s_ctx = jnp.einsum('bqkgd,blkd->bkgql', qblk, k_ctx, preferred_element_type=jnp.float32) * scale
        s_loc = jnp.einsum('bqkgd,bmkd->bkgqm', qblk, kblk, preferred_element_type=jnp.float32) * scale
        s_loc = jnp.where(vmask[None, None, None], s_loc, NEG_INF)
        p = attn_probs(jnp.concatenate([s_ctx, s_loc], axis=-1), sink).astype(v.dtype)
        o = (jnp.einsum('bkgql,blkd->bqkgd', p[..., :L], v_ctx, preferred_element_type=jnp.float32)
             + jnp.einsum('bkgqm,bmkd->bqkgd', p[..., L:], vblk, preferred_element_type=jnp.float32))
        return o.astype(q.dtype)

    out = lax.map(one_block, (qb, band(k), band(v), valid))
    return jnp.moveaxis(out, 0, 1).reshape(B, S, KH * G * Dh)


def chunk_spatial_gating(u, v, ln_g, ln_b, w_s, b_s):
    B, S, C = u.shape
    nc = S // GM_CHUNK
    u = jax.nn.gelu(u)
    vg = layer_norm(jax.nn.gelu(v), ln_g, ln_b).reshape(B, nc, GM_CHUNK, GM_GROUPS, GM_GROUP_DIM)
    mixed = jnp.einsum('gpq,bnqgc->bnpgc', w_s, vg) + b_s.T[None, None, :, :, None]
    return u * mixed.reshape(B, S, C)


def mla_keys(ckv, kr, p):
    B, M, _ = ckv.shape
    kv = (ckv @ p['mla_w_ukv']).reshape(B, M, MLA_HEADS, MLA_NOPE + MLA_V)
    k_rope = jnp.broadcast_to(kr[:, :, None, :], (B, M, MLA_HEADS, MLA_ROPE))
    k = rms_norm(jnp.concatenate([kv[..., :MLA_NOPE], k_rope], axis=-1), p['mla_k_norm'])
    return k, kv[..., MLA_NOPE:]


def rope_tail(t, cos, sin):
    return jnp.concatenate([t[..., :MLA_NOPE], apply_axial_rope(t[..., MLA_NOPE:], cos, sin)], axis=-1)


def token_stage(x, shift, scale, p):
    B, S, _ = x.shape
    h = modulate(rms_norm(x, p['g_mix']), shift, scale)
    qa, ka, va, cq, ckv, kr, u, v, ga, gb, gc = split_columns(h @ p['w_in'])
    qa = rms_norm(qa.reshape(B, S, SWA_KV_HEADS, SWA_GROUP, SWA_HEAD_DIM), p['swa_q_norm'])
    ka = rms_norm(ka.reshape(B, S, SWA_KV_HEADS, SWA_HEAD_DIM), p['swa_k_norm'])
    va = va.reshape(B, S, SWA_KV_HEADS, SWA_HEAD_DIM)
    qb = rms_norm(cq, p['mla_cq_norm']) @ p['mla_w_uq']
    qb = rms_norm(qb.reshape(B, S, MLA_HEADS, 1, MLA_QK), p['mla_q_norm'])
    ckv = rms_norm(ckv, p['mla_ckv_norm'])
    oc = chunk_spatial_gating(u, v, p['gm_ln_g'], p['gm_ln_b'], p['gm_w_s'], p['gm_b_s'])
    return qa, ka, va, qb, ckv, kr, oc, (ga, gb, gc)


def hier_moe(h, p):
    B, S, D = h.shape
    x = h.reshape(B * S, D)
    g_prob = jax.nn.softmax((x @ p['w_rg'] + p['b_rg']).astype(jnp.float32), axis=-1)
    g_w, g_idx = lax.top_k(g_prob, 1)
    e_logits = (x @ p['w_re'] + p['b_re']).astype(jnp.float32).reshape(-1, N_EXPERT_GROUPS, EXPERTS_PER_GROUP)
    e_logits = jnp.einsum('tn,tne->te', jax.nn.one_hot(g_idx[:, 0], N_EXPERT_GROUPS, dtype=jnp.float32), e_logits)
    e_w, e_idx = lax.top_k(jax.nn.softmax(e_logits, axis=-1), MOE_TOP_K)
    e_w = e_w / jnp.sum(e_w, axis=-1, keepdims=True)
    combine = jnp.sum(jax.nn.one_hot(g_idx * EXPERTS_PER_GROUP + e_idx, N_EXPERTS, dtype=jnp.float32)
                      * (g_w * e_w)[..., None], axis=1)
    y = jnp.zeros((B * S, D), jnp.float32)
    for e in range(N_EXPERTS):
        hid = jax.nn.silu(x @ p['w_gate'][e]) * (x @ p['w_up'][e])
        y = y + combine[:, e:e + 1] * (hid @ p['w_down'][e]).astype(jnp.float32)
    return y.reshape(B, S, D).astype(h.dtype)


def finish_layer(x, oa, ob, oc, gates, gate1, shift2, scale2, gate2, p):
    ga, gb, gc = gates
    merged = (jax.nn.sigmoid(ga) * (oa @ p['w_o_a'])
              + jax.nn.sigmoid(gb) * (ob @ p['w_o_b'])
              + jax.nn.sigmoid(gc) * (oc @ p['w_o_c']))
    x = x + gate1[:, None, :] * (merged @ p['w_out'])
    h = modulate(rms_norm(x, p['g_ffn']), shift2, scale2)
    return x + gate2[:, None, :] * hier_moe(h, p)


def context_layer(x, mod, p):
    shift1, scale1, gate1, shift2, scale2, gate2 = mod
    qa, ka, va, qb, ckv, kr, oc, gates = token_stage(x, shift1, scale1, p)
    oa = blocked_attention(qa, ka, va, SWA_HEAD_DIM ** -0.5, p['swa_sink'])
    kb, vb = mla_keys(ckv, kr, p)
    ob = blocked_attention(qb, kb, vb, MLA_QK ** -0.5)
    y = finish_layer(x, oa, ob, oc, gates, gate1, shift2, scale2, gate2, p)
    return y, (ka, va, ckv, kr)


def latent_layer(x, mod, p, k_ctx, v_ctx, ckv_ctx, kr_ctx):
    shift1, scale1, gate1, shift2, scale2, gate2 = mod
    S = x.shape[1]
    qa, ka, va, qb, ckv, kr, oc, gates = token_stage(x, shift1, scale1, p)
    cos_a, sin_a = axial_rope_tables(S, SWA_HEAD_DIM)
    oa = window_attention(apply_axial_rope(qa, cos_a, sin_a), apply_axial_rope(ka, cos_a, sin_a), va,
                          k_ctx, v_ctx, p['swa_sink'], SWA_HEAD_DIM ** -0.5)
    cos_b, sin_b = axial_rope_tables(S, MLA_ROPE)
    kb, vb = mla_keys(ckv, kr, p)
    kb_c, vb_c = mla_keys(ckv_ctx, kr_ctx, p)
    ob = blocked_attention(rope_tail(qb, cos_b, sin_b),
                           jnp.concatenate([kb_c, rope_tail(kb, cos_b, sin_b)], axis=1),
                           jnp.concatenate([vb_c, vb], axis=1), MLA_QK ** -0.5)
    return finish_layer(x, oa, ob, oc, gates, gate1, shift2, scale2, gate2, p)


def setup_inputs(seed: int = 0) -> dict:
    key = jax.random.key(seed)
    keys = iter(jax.random.split(key, 48))
    D = D_MODEL

    def nrm(shape, scale=1.0):
        return jax.random.normal(next(keys), shape, jnp.float32) * scale

    def gain(shape):
        return 1.0 + nrm(shape, 0.02)

    return {
        'x_prompt': nrm((BATCH, SEQ, D)),
        'x_sample': nrm((DEC_BATCH, DEC_SEQ, D)),
        'cache_swa_k': nrm((DEC_BATCH, DEPTH, PAST_LEN, SWA_KV_HEADS, SWA_HEAD_DIM)),
        'cache_swa_v': nrm((DEC_BATCH, DEPTH, PAST_LEN, SWA_KV_HEADS, SWA_HEAD_DIM)),
        'cache_mla_ckv': nrm((DEC_BATCH, DEPTH, PAST_LEN, MLA_KV_LORA)),
        'cache_mla_krope': nrm((DEC_BATCH, DEPTH, PAST_LEN, MLA_ROPE)),
        'c': nrm((DEC_BATCH, D)),
        'c_ctx': nrm((D,)),
        'w_mod': nrm((DEPTH, D, 6 * D), 0.5 * D ** -0.5),
        'b_mod': nrm((DEPTH, 6 * D), 0.02),
        'g_mix': gain((DEPTH, D)),
        'g_ffn': gain((DEPTH, D)),
        'w_in': nrm((DEPTH, D, IN_COLS), D ** -0.5),
        'swa_q_norm': gain((DEPTH, SWA_HEAD_DIM)),
        'swa_k_norm': gain((DEPTH, SWA_HEAD_DIM)),
        'swa_sink': nrm((DEPTH, SWA_KV_HEADS, SWA_GROUP), 0.5),
        'mla_cq_norm': gain((DEPTH, MLA_Q_LORA)),
        'mla_ckv_norm': gain((DEPTH, MLA_KV_LORA)),
        'mla_w_uq': nrm((DEPTH, MLA_Q_LORA, MLA_HEADS * MLA_QK), MLA_Q_LORA ** -0.5),
        'mla_w_ukv': nrm((DEPTH, MLA_KV_LORA, MLA_HEADS * (MLA_NOPE + MLA_V)), MLA_KV_LORA ** -0.5),
        'mla_q_norm': gain((DEPTH, MLA_QK)),
        'mla_k_norm': gain((DEPTH, MLA_QK)),
        'gm_ln_g': gain((DEPTH, GM_WIDTH)),
        'gm_ln_b': nrm((DEPTH, GM_WIDTH), 0.02),
        'gm_w_s': nrm((DEPTH, GM_GROUPS, GM_CHUNK, GM_CHUNK), GM_CHUNK ** -0.5),
        'gm_b_s': 1.0 + nrm((DEPTH, GM_GROUPS, GM_CHUNK), 0.02),
        'w_o_a': nrm((DEPTH, SWA_Q_COLS, D), SWA_Q_COLS ** -0.5),
        'w_o_b': nrm((DEPTH, MLA_HEADS * MLA_V, D), (MLA_HEADS * MLA_V) ** -0.5),
        'w_o_c': nrm((DEPTH, GM_WIDTH, D), GM_WIDTH ** -0.5),
        'w_out': nrm((DEPTH, D, D), D ** -0.5),
        'w_rg': nrm((DEPTH, D, N_EXPERT_GROUPS), D ** -0.5),
        'b_rg': nrm((DEPTH, N_EXPERT_GROUPS), 0.01),
        'w_re': nrm((DEPTH, D, N_EXPERTS), D ** -0.5),
        'b_re': nrm((DEPTH, N_EXPERTS), 0.01),
        'w_gate': nrm((DEPTH, N_EXPERTS, D, EXPERT_FF), D ** -0.5),
        'w_up': nrm((DEPTH, N_EXPERTS, D, EXPERT_FF), D ** -0.5),
        'w_down': nrm((DEPTH, N_EXPERTS, EXPERT_FF, D), EXPERT_FF ** -0.5),
    }


def reference(x_prompt, x_sample, cache_swa_k, cache_swa_v, cache_mla_ckv, cache_mla_krope, c, c_ctx,
              w_mod, b_mod, g_mix, g_ffn, w_in, swa_q_norm, swa_k_norm, swa_sink,
              mla_cq_norm, mla_ckv_norm, mla_w_uq, mla_w_ukv, mla_q_norm, mla_k_norm,
              gm_ln_g, gm_ln_b, gm_w_s, gm_b_s, w_o_a, w_o_b, w_o_c, w_out,
              w_rg, b_rg, w_re, b_re, w_gate, w_up, w_down):
    y_p, y_s = x_prompt, x_sample
    ks, vs, ckvs, krs = [], [], [], []
    for l in range(DEPTH):
        p = {
            'g_mix': g_mix[l], 'g_ffn': g_ffn[l], 'w_in': w_in[l],
            'swa_q_norm': swa_q_norm[l], 'swa_k_norm': swa_k_norm[l], 'swa_sink': swa_sink[l],
            'mla_cq_norm': mla_cq_norm[l], 'mla_ckv_norm': mla_ckv_norm[l],
            'mla_w_uq': mla_w_uq[l], 'mla_w_ukv': mla_w_ukv[l],
            'mla_q_norm': mla_q_norm[l], 'mla_k_norm': mla_k_norm[l],
            'gm_ln_g': gm_ln_g[l], 'gm_ln_b': gm_ln_b[l], 'gm_w_s': gm_w_s[l], 'gm_b_s': gm_b_s[l],
            'w_o_a': w_o_a[l], 'w_o_b': w_o_b[l], 'w_o_c': w_o_c[l], 'w_out': w_out[l],
            'w_rg': w_rg[l], 'b_rg': b_rg[l], 'w_re': w_re[l], 'b_re': b_re[l],
            'w_gate': w_gate[l], 'w_up': w_up[l], 'w_down': w_down[l],
        }
        mod_ctx = adaln(c_ctx[None, :], w_mod[l], b_mod[l])
        mod_lat = adaln(c, w_mod[l], b_mod[l])
        y_p, (k_l, v_l, ckv_l, kr_l) = context_layer(y_p, mod_ctx, p)
        ks.append(k_l)
        vs.append(v_l)
        ckvs.append(ckv_l)
        krs.append(kr_l)
        y_s = latent_layer(y_s, mod_lat, p, cache_swa_k[:, l], cache_swa_v[:, l], cache_mla_ckv[:, l], cache_mla_krope[:, l])
    new_swa_k = jnp.stack(ks, axis=1)
    new_swa_v = jnp.stack(vs, axis=1)
    new_mla_ckv = jnp.stack(ckvs, axis=1)
    new_mla_krope = jnp.stack(krs, axis=1)
    return (y_p, y_s, new_swa_k, new_swa_v, new_mla_ckv, new_mla_krope)
```

```python
import functools

import numpy as np
import jax
import jax.numpy as jnp
from jax import lax
from jax.experimental import pallas as pl
from jax.experimental.pallas import tpu as pltpu

F32 = jnp.float32
BF16 = jnp.bfloat16

D_MODEL = 1024
GRID_W = 64
ROPE_THETA = 10000.0
EPS = 1e-6
NEG_INF = -1e30

SWA_HEADS = 8
SWA_KV_HEADS = 2
SWA_GROUP = SWA_HEADS // SWA_KV_HEADS
SWA_HEAD_DIM = 64
SWA_WINDOW = 128
SWA_Q_COLS = SWA_HEADS * SWA_HEAD_DIM
SWA_KV_COLS = SWA_KV_HEADS * SWA_HEAD_DIM

MLA_HEADS = 8
MLA_Q_LORA = 256
MLA_KV_LORA = 128
MLA_NOPE = 64
MLA_ROPE = 32
MLA_V = 64
MLA_QK = MLA_NOPE + MLA_ROPE

GM_CHUNK = 128
GM_GROUPS = 4
GM_WIDTH = 512

N_EXPERT_GROUPS = 4
EXPERTS_PER_GROUP = 4
N_EXPERTS = N_EXPERT_GROUPS * EXPERTS_PER_GROUP
EXPERT_FF = 512

LANES = 128
MXU_DIM = 256
VMEM_LIMIT_BYTES = 56 * 1024 * 1024

HEAD_PAD = LANES
MLA_PAD_COLS = MLA_HEADS * HEAD_PAD

TOKEN_TILE = 256
MLA_Q_TILE = 256
MOE_TILE_CANDIDATES = (1024, 512, 256)

_QA0, _KA0, _VA0, _CQ0, _CKV0, _KR0, _U0, _V0, _ZCOLS = 0, 512, 640, 768, 1024, 1152, 1280, 1792, 2304
_GATE0 = SWA_Q_COLS + 2 * SWA_KV_COLS + MLA_Q_LORA + MLA_KV_LORA + MLA_ROPE + 2 * GM_WIDTH

ROUTER_LANE0 = N_EXPERT_GROUPS


def _cparams(*sem):
    return pltpu.CompilerParams(dimension_semantics=sem, vmem_limit_bytes=VMEM_LIMIT_BYTES)


def _nt_dot(a, b):
    return lax.dot_general(a, b, (((1,), (1,)), ((), ())), preferred_element_type=F32)


def _dot(a, b):
    return jnp.dot(a, b, preferred_element_type=F32)


def _gelu_tanh(x):
    return 0.5 * x * (1.0 + jnp.tanh(np.sqrt(2.0 / np.pi) * (x + 0.044715 * (x * x * x))))


def _group_sumsq(v, ones_ref):
    width = ones_ref.shape[0]
    parts = []
    for c in range(v.shape[1] // width):
        blk = v[:, c * width:(c + 1) * width]
        parts.append(_dot((blk * blk).astype(BF16), ones_ref[...]))
    return parts[0] if len(parts) == 1 else jnp.concatenate(parts, axis=1)


def _rope(v, cos, sin, rot_ref):
    width = rot_ref.shape[0]
    parts = []
    for c in range(v.shape[1] // width):
        blk = v[:, c * width:(c + 1) * width]
        rot = _dot(blk.astype(BF16), rot_ref[...])
        for s in range(width // LANES):
            sl = slice(s * LANES, (s + 1) * LANES)
            parts.append(blk[:, sl] * cos + rot[:, sl] * sin)
    return parts[0] if len(parts) == 1 else jnp.concatenate(parts, axis=1)


def _modulated_norm(x, gain, shift, scale):
    ms = jnp.mean(x * x, axis=-1, keepdims=True)
    h = x * lax.rsqrt(ms + EPS) * gain
    return h * (1.0 + scale) + shift


def _adaln_kernel(cond_ref, w_ref, b_ref, o_ref):
    a = cond_ref[...]
    a = a * jax.nn.sigmoid(a)
    o_ref[...] = _dot(a.astype(BF16), w_ref[...].astype(BF16)) + b_ref[...]


def _adaln(cond8, w_mod, b_mod):
    depth, d, n = w_mod.shape
    tn = 1536
    return pl.pallas_call(
        _adaln_kernel,
        grid=(depth, n // tn),
        in_specs=[
            pl.BlockSpec((8, d), lambda l, j: (0, 0)),
            pl.BlockSpec((None, d, tn), lambda l, j: (l, 0, j)),
            pl.BlockSpec((None, 1, tn), lambda l, j: (l, 0, j)),
        ],
        out_specs=pl.BlockSpec((None, 8, tn), lambda l, j: (l, 0, j)),
        out_shape=jax.ShapeDtypeStruct((depth, 8, n), F32),
        compiler_params=_cparams("parallel", "parallel"),
        name="adaln",
    )(cond8, w_mod, b_mod.reshape(depth, 1, n))


def _mla_keys_values(ckv_n, kr_hi, wkn_ref, wv_ref, gk1_ref, gk2_ref, ones128_ref, rope_args):
    cb = ckv_n.astype(BF16)
    nope = _dot(cb, wkn_ref[...])
    vals = _dot(cb, wv_ref[...])
    kr_ss = jnp.sum(kr_hi * kr_hi, axis=-1, keepdims=True)
    inv = lax.rsqrt((_group_sumsq(nope, ones128_ref) + kr_ss) * (1.0 / MLA_QK) + EPS)
    krg = kr_hi * gk2_ref[...]
    if rope_args is not None:
        cos, sin, rot_ref = rope_args
        krg = _rope(krg, cos, sin, rot_ref)
    gk1 = gk1_ref[...]
    parts = []
    for h in range(MLA_HEADS):
        sl = slice(h * HEAD_PAD, (h + 1) * HEAD_PAD)
        parts.append((nope[:, sl] * gk1[:, sl] + krg) * inv[:, sl])
    return jnp.concatenate(parts, axis=1), vals


def _ctx_keys_kernel(ckv_ref, kr_ref, wkn_ref, wv_ref, gk1_ref, gk2_ref, ones128_ref, k_ref, v_ref):
    keys, vals = _mla_keys_values(ckv_ref[...], kr_ref[...], wkn_ref, wv_ref, gk1_ref, gk2_ref,
                                  ones128_ref, None)
    k_ref[...] = keys.astype(BF16)
    v_ref[...] = vals.astype(BF16)


def _ctx_keys(cache_ckv, cache_kr_hi, wkn, wv, gk1, gk2, ones128):
    nb, depth, past, _ = cache_ckv.shape
    cache_map = lambda l, b: (b, l, 0, 0)
    w_map = lambda l, b: (l, 0, 0)
    return pl.pallas_call(
        _ctx_keys_kernel,
        grid=(depth, nb),
        in_specs=[
            pl.BlockSpec((None, None, past, MLA_KV_LORA), cache_map),
            pl.BlockSpec((None, None, past, LANES), cache_map),
            pl.BlockSpec((None, MLA_KV_LORA, MLA_PAD_COLS), w_map),
            pl.BlockSpec((None, MLA_KV_LORA, MLA_HEADS * MLA_V), w_map),
            pl.BlockSpec((None, 1, MLA_PAD_COLS), w_map),
            pl.BlockSpec((None, 1, LANES), w_map),
            pl.BlockSpec((MXU_DIM, MXU_DIM), lambda l, b: (0, 0)),
        ],
        out_specs=[
            pl.BlockSpec((None, None, past, MLA_PAD_COLS), lambda l, b: (l, b, 0, 0)),
            pl.BlockSpec((None, None, past, MLA_HEADS * MLA_V), lambda l, b: (l, b, 0, 0)),
        ],
        out_shape=[
            jax.ShapeDtypeStruct((depth, nb, past, MLA_PAD_COLS), BF16),
            jax.ShapeDtypeStruct((depth, nb, past, MLA_HEADS * MLA_V), BF16),
        ],
        compiler_params=_cparams("parallel", "parallel"),
        name="ctx_keys",
    )(cache_ckv, cache_kr_hi, wkn, wv, gk1, gk2, ones128)


def _token_kernel(latent, x_ref, mod_ref, gmix_ref, wa_ref, gqa_ref, gka_ref, gcq_ref, wuq_ref, gq_ref,
                  gckv_ref, wkn_ref, wv_ref, gk1_ref, gk2_ref, lng_ref, lnb_ref, ws_ref, bs_ref,
                  ones64_ref, ones128_ref, *rest):
    if latent:
        (cosa_ref, sina_ref, cosb_ref, sinb_ref, rota_ref, rotb_ref,
         qa_o, kd_o, va_o, qb_o, kb_o, vb_o, oc_o) = rest
    else:
        (qa_o, kd_o, va_o, qb_o, kb_o, vb_o, oc_o, kac_o, vac_o, ckvc_o, krc_o) = rest
    d = D_MODEL
    rows = x_ref.shape[0]
    h = _modulated_norm(x_ref[...], gmix_ref[...], mod_ref[:, 0:d], mod_ref[:, d:2 * d])
    z = _dot(h.astype(BF16), wa_ref[...])

    qa = z[:, _QA0:_KA0]
    qa = qa * lax.rsqrt(_group_sumsq(qa, ones64_ref) * (1.0 / SWA_HEAD_DIM) + EPS) * gqa_ref[...]
    ka = z[:, _KA0:_VA0]
    ka_ss = _dot((ka * ka).astype(BF16), ones64_ref[0:LANES, 0:LANES])
    ka = ka * lax.rsqrt(ka_ss * (1.0 / SWA_HEAD_DIM) + EPS) * gka_ref[...]
    va = z[:, _VA0:_CQ0]
    if latent:
        qa = _rope(qa, cosa_ref[...], sina_ref[...], rota_ref)
        ka_r = _rope(ka, cosa_ref[...], sina_ref[...], rota_ref.at[0:LANES, 0:LANES])
    else:
        kac_o[...] = ka
        vac_o[...] = va
        ka_r = ka
    swapped = pltpu.roll(ka_r, SWA_HEAD_DIM, 1)
    low = lax.broadcasted_iota(jnp.int32, (rows, LANES), 1) < SWA_HEAD_DIM
    qa_o[...] = qa.astype(BF16)
    kd_o[:, 0:LANES] = jnp.where(low, ka_r, swapped).astype(BF16)
    kd_o[:, LANES:2 * LANES] = jnp.where(low, swapped, ka_r).astype(BF16)
    va_o[...] = va.astype(BF16)

    cq = z[:, _CQ0:_CKV0]
    cq = cq * lax.rsqrt(jnp.mean(cq * cq, axis=-1, keepdims=True) + EPS) * gcq_ref[...]
    qb = _dot(cq.astype(BF16), wuq_ref[...])
    qb = qb * lax.rsqrt(_group_sumsq(qb, ones128_ref) * (1.0 / MLA_QK) + EPS) * gq_ref[...]
    if latent:
        qb = _rope(qb, cosb_ref[...], sinb_ref[...], rotb_ref)
    qb_o[...] = qb.astype(BF16)

    ckv = z[:, _CKV0:_KR0]
    ckv = ckv * lax.rsqrt(jnp.mean(ckv * ckv, axis=-1, keepdims=True) + EPS) * gckv_ref[...]
    krb = z[:, _KR0:_U0]
    lane = lax.broadcasted_iota(jnp.int32, (rows, LANES), 1)
    kr_hi = jnp.where(lane >= MLA_NOPE, krb, 0.0)
    rope_args = (cosb_ref[...], sinb_ref[...], rotb_ref.at[0:LANES, 0:LANES]) if latent else None
    kb, vb = _mla_keys_values(ckv, kr_hi, wkn_ref, wv_ref, gk1_ref, gk2_ref, ones128_ref, rope_args)
    kb_o[...] = kb.astype(BF16)
    vb_o[...] = vb.astype(BF16)
    if not latent:
        ckvc_o[...] = ckv
        krc_o[...] = krb[:, 0:MLA_ROPE]

    gu = _gelu_tanh(z[:, _U0:_V0])
    gv = _gelu_tanh(z[:, _V0:_ZCOLS])
    gc = gv - jnp.mean(gv, axis=-1, keepdims=True)
    var = jnp.mean(gc * gc, axis=-1, keepdims=True)
    vg = (gc * lax.rsqrt(var + EPS) * lng_ref[...] + lnb_ref[...]).astype(BF16)
    n_chunks = rows // GM_CHUNK
    for g in range(GM_GROUPS):
        gl = slice(g * LANES, (g + 1) * LANES)
        rhs = jnp.concatenate([vg[c * GM_CHUNK:(c + 1) * GM_CHUNK, gl] for c in range(n_chunks)], axis=1)
        mixed = _dot(ws_ref[g], rhs)
        for c in range(n_chunks):
            rs = slice(c * GM_CHUNK, (c + 1) * GM_CHUNK)
            oc_o[rs, gl] = (gu[rs, gl] * (mixed[:, c * LANES:(c + 1) * LANES] + bs_ref[:, gl])).astype(BF16)


def _token_stage(latent, x_all, tile0, n_tiles, mod_l, mod_row, wl, consts, tables):
    t = TOKEN_TILE
    d = D_MODEL
    n = n_tiles * t
    full = lambda a: pl.BlockSpec(a.shape, lambda i: (0,) * a.ndim)
    weights = [wl["g_mix"], wl["w_a"], wl["gqa"], wl["gka"], wl["gcq"], wl["w_uq"], wl["gq"], wl["gckv"],
               wl["w_kn"], wl["w_v"], wl["gk1"], wl["gk2"], wl["ln_g"], wl["ln_b"], wl["w_s"], wl["b_s"],
               consts["ones64"], consts["ones128"]]
    in_specs = [
        pl.BlockSpec((t, d), lambda i: (i + tile0, 0)),
        pl.BlockSpec((None, 1, 6 * d), lambda i: (mod_row(i), 0, 0)),
    ] + [full(a) for a in weights]
    args = [x_all, mod_l] + weights
    row = lambda w: pl.BlockSpec((t, w), lambda i: (i, 0))
    out_widths = [SWA_Q_COLS, 2 * LANES, LANES, MLA_PAD_COLS, MLA_PAD_COLS, MLA_HEADS * MLA_V, GM_WIDTH]
    out_specs = [row(w) for w in out_widths]
    out_shape = [jax.ShapeDtypeStruct((n, w), BF16) for w in out_widths]
    if latent:
        tiles_per_seq = tables["cos_a"].shape[0] // t
        tab = lambda: pl.BlockSpec((t, LANES), lambda i: (i % tiles_per_seq, 0))
        in_specs += [tab(), tab(), tab(), tab(), full(consts["rot_a"]), full(consts["rot_b"])]
        args += [tables["cos_a"], tables["sin_a"], tables["cos_b"], tables["sin_b"],
                 consts["rot_a"], consts["rot_b"]]
    else:
        cache_widths = [SWA_KV_COLS, SWA_KV_COLS, MLA_KV_LORA, MLA_ROPE]
        out_specs += [row(w) for w in cache_widths]
        out_shape += [jax.ShapeDtypeStruct((n, w), F32) for w in cache_widths]
    return pl.pallas_call(
        functools.partial(_token_kernel, latent),
        grid=(n_tiles,),
        in_specs=in_specs,
        out_specs=out_specs,
        out_shape=out_shape,
        compiler_params=_cparams("parallel"),
        name="token_stage_latent" if latent else "token_stage_ctx",
    )(*args)


def _softmax_pv(scores, values, extra=None):
    m = jnp.max(scores[0], axis=-1, keepdims=True)
    for s in scores[1:]:
        m = jnp.maximum(m, jnp.max(s, axis=-1, keepdims=True))
    if extra is not None:
        m = jnp.maximum(m, extra)
    den = jnp.exp(extra - m) if extra is not None else 0.0
    acc = None
    for s, v in zip(scores, values):
        p = jnp.exp(s - m)
        den = den + jnp.sum(p, axis=-1, keepdims=True)
        pv = _dot(p.astype(BF16), v)
        acc = pv if acc is None else acc + pv
    return acc / den


def _ctx_attn_kernel(sink_ref, qa_ref, kd_ref, va_ref, qb_ref, kb_ref, vb_ref, oa_ref, ob_ref):
    rows = qa_ref.shape[0]
    low = lax.broadcasted_iota(jnp.int32, (rows, LANES), 1) < SWA_HEAD_DIM
    va = va_ref[...]
    res = []
    for hd in range(SWA_HEADS):
        qblk = qa_ref[:, (hd // 2) * LANES:(hd // 2 + 1) * LANES]
        keep = low if hd % 2 == 0 else jnp.logical_not(low)
        qm = jnp.where(keep, qblk, jnp.zeros_like(qblk))
        kvh = hd // SWA_GROUP
        s = _nt_dot(qm, kd_ref[:, kvh * LANES:(kvh + 1) * LANES]) * (SWA_HEAD_DIM ** -0.5)
        res.append(_softmax_pv([s], [va], sink_ref[hd]))
    for j in range(SWA_GROUP):
        oa_ref[:, j * LANES:(j + 1) * LANES] = jnp.where(low, res[j], res[SWA_GROUP + j]).astype(BF16)
    res = []
    for h in range(MLA_HEADS):
        sl = slice(h * HEAD_PAD, (h + 1) * HEAD_PAD)
        s = _nt_dot(qb_ref[:, sl], kb_ref[:, sl]) * (MLA_QK ** -0.5)
        res.append(_softmax_pv([s], [vb_ref[:, (h // 2) * LANES:(h // 2 + 1) * LANES]]))
    for j in range(MLA_HEADS // 2):
        ob_ref[:, j * LANES:(j + 1) * LANES] = jnp.where(low, res[2 * j], res[2 * j + 1]).astype(BF16)


def _ctx_attention(sink, qa, kd, va, qb, kb, vb, n_seq, seq):
    row = lambda w: pl.BlockSpec((seq, w), lambda b: (b, 0))
    widths = [SWA_Q_COLS, 2 * LANES, LANES, MLA_PAD_COLS, MLA_PAD_COLS, MLA_HEADS * MLA_V]
    return pl.pallas_call(
        _ctx_attn_kernel,
        grid=(n_seq,),
        in_specs=[pl.BlockSpec(memory_space=pltpu.SMEM)] + [row(w) for w in widths],
        out_specs=[row(SWA_Q_COLS), row(MLA_HEADS * MLA_V)],
        out_shape=[jax.ShapeDtypeStruct((n_seq * seq, SWA_Q_COLS), BF16),
                   jax.ShapeDtypeStruct((n_seq * seq, MLA_HEADS * MLA_V), BF16)],
        compiler_params=_cparams("parallel"),
        name="ctx_attention",
    )(sink, qa, kd, va, qb, kb, vb)


def _window_attn_kernel(sink_ref, q_ref, kp_ref, kc_ref, kn_ref, vp_ref, vc_ref, vn_ref, kctx_ref, vctx_ref,
                        o_ref):
    w = SWA_WINDOW
    i = pl.program_id(1)
    nb = pl.num_programs(1)
    low = lax.broadcasted_iota(jnp.int32, (w, LANES), 1) < SWA_HEAD_DIM
    rows = SWA_GROUP * w
    r_io = lax.broadcasted_iota(jnp.int32, (rows, 3 * w), 0)
    c_io = lax.broadcasted_iota(jnp.int32, (rows, 3 * w), 1)
    rel = c_io - (r_io & (w - 1))
    first_col = jnp.where(i == 0, w, 0)
    end_col = jnp.where(i == nb - 1, 2 * w, 3 * w)
    valid = (rel >= 0) & (rel <= 2 * w) & (c_io >= first_col) & (c_io < end_col)
    g_io = lax.broadcasted_iota(jnp.int32, (rows, 1), 0) >> (w.bit_length() - 1)
    v_loc = jnp.concatenate([vp_ref[...], vc_ref[...], vn_ref[...]], axis=0)
    v_ctx = vctx_ref[...]
    res = []
    for kvh in range(SWA_KV_HEADS):
        qs = []
        sink = jnp.zeros((rows, 1), F32)
        for g in range(SWA_GROUP):
            hd = kvh * SWA_GROUP + g
            qblk = q_ref[:, (hd // 2) * LANES:(hd // 2 + 1) * LANES]
            keep = low if hd % 2 == 0 else jnp.logical_not(low)
            qs.append(jnp.where(keep, qblk, jnp.zeros_like(qblk)))
            sink = jnp.where(g_io == g, sink_ref[hd], sink)
        qs = jnp.concatenate(qs, axis=0)
        ks = slice(kvh * LANES, (kvh + 1) * LANES)
        k_loc = jnp.concatenate([kp_ref[:, ks], kc_ref[:, ks], kn_ref[:, ks]], axis=0)
        s_ctx = _nt_dot(qs, kctx_ref[:, ks]) * (SWA_HEAD_DIM ** -0.5)
        s_loc = _nt_dot(qs, k_loc) * (SWA_HEAD_DIM ** -0.5)
        s_loc = jnp.where(valid, s_loc, NEG_INF)
        res.append(_softmax_pv([s_ctx, s_loc], [v_ctx, v_loc], sink))
    for j in range(SWA_GROUP):
        rs = slice(j * w, (j + 1) * w)
        o_ref[:, j * LANES:(j + 1) * LANES] = jnp.where(low, res[0][rs], res[1][rs]).astype(BF16)


def _window_attention(sink, qa, kd, va, kd_ctx, v_ctx, n_seq, seq):
    w = SWA_WINDOW
    nb = seq // w
    past = kd_ctx.shape[1]
    cur = lambda b, i: (b * nb + i, 0)
    prev = lambda b, i: (b * nb + jnp.maximum(i - 1, 0), 0)
    nxt = lambda b, i: (b * nb + jnp.minimum(i + 1, nb - 1), 0)
    return pl.pallas_call(
        _window_attn_kernel,
        grid=(n_seq, nb),
        in_specs=[
            pl.BlockSpec(memory_space=pltpu.SMEM),
            pl.BlockSpec((w, SWA_Q_COLS), cur),
            pl.BlockSpec((w, 2 * LANES), prev),
            pl.BlockSpec((w, 2 * LANES), cur),
            pl.BlockSpec((w, 2 * LANES), nxt),
            pl.BlockSpec((w, LANES), prev),
            pl.BlockSpec((w, LANES), cur),
            pl.BlockSpec((w, LANES), nxt),
            pl.BlockSpec((None, past, 2 * LANES), lambda b, i: (b, 0, 0)),
            pl.BlockSpec((None, past, LANES), lambda b, i: (b, 0, 0)),
        ],
        out_specs=pl.BlockSpec((w, SWA_Q_COLS), cur),
        out_shape=jax.ShapeDtypeStruct((n_seq * seq, SWA_Q_COLS), BF16),
        compiler_params=_cparams("parallel", "parallel"),
        name="window_attention",
    )(sink, qa, kd, kd, kd, va, va, va, kd_ctx, v_ctx)


def _latent_mla_kernel(q_ref, k_ref, v_ref, kctx_ref, vctx_ref, o_ref):
    rows = q_ref.shape[0]
    low = lax.broadcasted_iota(jnp.int32, (rows, LANES), 1) < MLA_V
    v_lat = v_ref[...]
    v_ctx = vctx_ref[...]
    res = []
    for e in range(2):
        sl = slice(e * HEAD_PAD, (e + 1) * HEAD_PAD)
        q = q_ref[:, sl]
        s_ctx = _nt_dot(q, kctx_ref[:, sl]) * (MLA_QK ** -0.5)
        s_lat = _nt_dot(q, k_ref[:, sl]) * (MLA_QK ** -0.5)
        res.append(_softmax_pv([s_ctx, s_lat], [v_ctx, v_lat]))
    o_ref[...] = jnp.where(low, res[0], res[1]).astype(BF16)


def _latent_mla(qb, kb, vb, k_ctx, v_ctx, n_seq, seq):
    tq = MLA_Q_TILE
    nq = seq // tq
    past = k_ctx.shape[1]
    pair = 2 * HEAD_PAD
    return pl.pallas_call(
        _latent_mla_kernel,
        grid=(n_seq, MLA_HEADS // 2, nq),
        in_specs=[
            pl.BlockSpec((tq, pair), lambda b, p, i: (b * nq + i, p)),
            pl.BlockSpec((seq, pair), lambda b, p, i: (b, p)),
            pl.BlockSpec((seq, LANES), lambda b, p, i: (b, p)),
            pl.BlockSpec((None, past, pair), lambda b, p, i: (b, 0, p)),
            pl.BlockSpec((None, past, LANES), lambda b, p, i: (b, 0, p)),
        ],
        out_specs=pl.BlockSpec((tq, LANES), lambda b, p, i: (b * nq + i, p)),
        out_shape=jax.ShapeDtypeStruct((n_seq * seq, MLA_HEADS * MLA_V), BF16),
        compiler_params=_cparams("parallel", "parallel", "parallel"),
        name="latent_mla",
    )(qb, kb, vb, k_ctx, v_ctx)


def _route(logits):
    lane = lax.broadcasted_iota(jnp.int32, logits.shape, 1).astype(F32)
    big = float(4 * LANES)
    gmask = lane < N_EXPERT_GROUPS
    gmax = jnp.max(jnp.where(gmask, logits, -jnp.inf), axis=-1, keepdims=True)
    gsum = jnp.sum(jnp.where(gmask, jnp.exp(logits - gmax), 0.0), axis=-1, keepdims=True)
    g_w = 1.0 / gsum
    g_idx = jnp.min(jnp.where(gmask & (logits == gmax), lane, big), axis=-1, keepdims=True)
    first = ROUTER_LANE0 + EXPERTS_PER_GROUP * g_idx
    emask = (lane >= first) & (lane < first + EXPERTS_PER_GROUP)
    el = jnp.where(emask, logits, -jnp.inf)
    m1 = jnp.max(el, axis=-1, keepdims=True)
    i1 = jnp.min(jnp.where(emask & (el == m1), lane, big), axis=-1, keepdims=True)
    el2 = jnp.where(lane == i1, -jnp.inf, el)
    m2 = jnp.max(el2, axis=-1, keepdims=True)
    i2 = jnp.min(jnp.where(emask & (el2 == m2), lane, big), axis=-1, keepdims=True)
    esum = jnp.sum(jnp.where(emask, jnp.exp(logits - m1), 0.0), axis=-1, keepdims=True)
    p1 = 1.0 / esum
    p2 = jnp.exp(m2 - m1) / esum
    tot = p1 + p2
    return jnp.where(lane == i1, g_w * (p1 / tot), 0.0) + jnp.where(lane == i2, g_w * (p2 / tot), 0.0)


def _finish_kernel(x_ref, oa_ref, ob_ref, oc_ref, mod_ref, gmix_ref, gffn_ref, wg_ref, woa_ref, wob_ref,
                   woc_ref, wout_ref, wrh_ref, wrl_ref, br_ref, x1_ref, h2_ref, comb_ref):
    d = D_MODEL
    x = x_ref[...]
    h = _modulated_norm(x, gmix_ref[...], mod_ref[:, 0:d], mod_ref[:, d:2 * d]).astype(BF16)
    gates = _dot(h, wg_ref[...])
    merged = (jax.nn.sigmoid(gates[:, 0:d]) * _dot(oa_ref[...], woa_ref[...])
              + jax.nn.sigmoid(gates[:, d:2 * d]) * _dot(ob_ref[...], wob_ref[...])
              + jax.nn.sigmoid(gates[:, 2 * d:3 * d]) * _dot(oc_ref[...], woc_ref[...]))
    x1 = x + mod_ref[:, 2 * d:3 * d] * _dot(merged.astype(BF16), wout_ref[...])
    x1_ref[...] = x1
    h2 = _modulated_norm(x1, gffn_ref[...], mod_ref[:, 3 * d:4 * d], mod_ref[:, 4 * d:5 * d])
    h2_hi = h2.astype(BF16)
    h2_lo = (h2 - h2_hi.astype(F32)).astype(BF16)
    h2_ref[...] = h2_hi
    logits = (_dot(h2_hi, wrh_ref[...]) + _dot(h2_lo, wrh_ref[...]) + _dot(h2_hi, wrl_ref[...])) + br_ref[...]
    comb_ref[...] = _route(logits)


def _finish(x_all, oa, ob, oc, mod_l, mod_row, wl):
    t = TOKEN_TILE
    d = D_MODEL
    n = x_all.shape[0]
    full = lambda a: pl.BlockSpec(a.shape, lambda i: (0,) * a.ndim)
    row = lambda w: pl.BlockSpec((t, w), lambda i: (i, 0))
    weights = [wl["g_mix"], wl["g_ffn"], wl["w_g"], wl["w_o_a"], wl["w_o_b"], wl["w_o_c"], wl["w_out"],
               wl["w_r_hi"], wl["w_r_lo"], wl["b_r"]]
    return pl.pallas_call(
        _finish_kernel,
        grid=(n // t,),
        in_specs=[row(d), row(SWA_Q_COLS), row(MLA_HEADS * MLA_V), row(GM_WIDTH),
                  pl.BlockSpec((None, 1, 6 * d), lambda i: (mod_row(i), 0, 0))] + [full(a) for a in weights],
        out_specs=[row(d), row(d), row(LANES)],
        out_shape=[jax.ShapeDtypeStruct((n, d), F32), jax.ShapeDtypeStruct((n, d), BF16),
                   jax.ShapeDtypeStruct((n, LANES), F32)],
        compiler_params=_cparams("parallel"),
        name="finish",
    )(x_all, oa, ob, oc, mod_l, *weights)


def _moe_kernel(h_ref, comb_ref, x1_ref, mod_ref, wg_ref, wu_ref, wd_ref, o_ref):
    d = D_MODEL
    e = pl.program_id(1)

    @pl.when(e == 0)
    def _():
        o_ref[...] = jnp.zeros_like(o_ref)

    h = h_ref[...]
    a = _dot(h, wg_ref[...])
    hid = (a * jax.nn.sigmoid(a)) * _dot(h, wu_ref[...])
    y = _dot(hid.astype(BF16), wd_ref[...])
    comb = comb_ref[...]
    lane = lax.broadcasted_iota(jnp.int32, comb.shape, 1)
    c = jnp.sum(jnp.where(lane == e + ROUTER_LANE0, comb, 0.0), axis=-1, keepdims=True)
    o_ref[...] += c * y

    @pl.when(e == pl.num_programs(1) - 1)
    def _():
        o_ref[...] = x1_ref[...] + mod_ref[:, 5 * d:6 * d] * o_ref[...]


def _moe(h2, comb, x1, mod_l, mod_row, wl, tile):
    d = D_MODEL
    n = h2.shape[0]
    row = lambda w: pl.BlockSpec((tile, w), lambda i, e: (i, 0))
    return pl.pallas_call(
        _moe_kernel,
        grid=(n // tile, N_EXPERTS),
        in_specs=[row(d), row(LANES), row(d),
                  pl.BlockSpec((None, 1, 6 * d), lambda i, e: (mod_row(i), 0, 0)),
                  pl.BlockSpec((None, d, EXPERT_FF), lambda i, e: (e, 0, 0)),
                  pl.BlockSpec((None, d, EXPERT_FF), lambda i, e: (e, 0, 0)),
                  pl.BlockSpec((None, EXPERT_FF, d), lambda i, e: (e, 0, 0))],
        out_specs=row(d),
        out_shape=jax.ShapeDtypeStruct((n, d), F32),
        compiler_params=_cparams("parallel", "arbitrary"),
        name="moe_dense",
    )(h2, comb, x1, mod_l, wl["w_gate"], wl["w_up"], wl["w_down"])


def _block_ones(group):
    idx = np.arange(MXU_DIM)
    return jnp.asarray((idx[:, None] // group) == (idx[None, :] // group), BF16)


def _rotate_half_matrix(block, rot_start, rot_dim):
    half = rot_dim // 4
    r = np.zeros((MXU_DIM, MXU_DIM), np.float32)
    for j in range(MXU_DIM):
        o = j % block - rot_start
        if 0 <= o < rot_dim:
            partner = j + half if (o % (2 * half)) < half else j - half
            r[partner, j] = 1.0
    return jnp.asarray(r, BF16)


def _rope_tables(n_tokens, block, rot_start, rot_dim):
    half = rot_dim // 4
    tok = np.arange(n_tokens)
    pos = np.stack([tok // GRID_W, tok % GRID_W], axis=1).astype(np.float32)
    inv_freq = (ROPE_THETA ** (-np.arange(half, dtype=np.float32) / half)).astype(np.float32)
    lane = np.arange(LANES)
    o = lane % block - rot_start
    rot = (o >= 0) & (o < rot_dim)
    oc = np.where(rot, o, 0)
    axis = oc // (2 * half)
    freq = inv_freq[oc % half]
    sign = np.where((oc % (2 * half)) < half, -1.0, 1.0).astype(np.float32)
    ang = jnp.asarray(pos[:, axis]) * jnp.asarray(freq)[None, :]
    rot_j = jnp.asarray(rot)[None, :]
    cos = jnp.where(rot_j, jnp.cos(ang), 1.0)
    sin = jnp.where(rot_j, jnp.sin(ang) * jnp.asarray(sign)[None, :], 0.0)
    return cos.astype(F32), sin.astype(F32)


def _pad_heads(w, heads, width):
    lead = w.shape[:-1]
    w = w.reshape(lead + (heads, width))
    w = jnp.pad(w, [(0, 0)] * len(lead) + [(0, 0), (0, HEAD_PAD - width)])
    return w.reshape(lead + (heads * HEAD_PAD,))


def _prep_weights(p):
    depth = p["w_in"].shape[0]
    w_in = p["w_in"]
    c = np.cumsum((0, SWA_Q_COLS, SWA_KV_COLS, SWA_KV_COLS, MLA_Q_LORA, MLA_KV_LORA, MLA_ROPE, GM_WIDTH,
                   GM_WIDTH))
    kr = w_in[..., c[5]:c[6]]
    zero = jnp.zeros_like(kr)
    w_a = jnp.concatenate([w_in[..., c[0]:c[5]], kr, zero, kr, zero, w_in[..., c[6]:c[8]]], axis=-1)
    ukv = p["mla_w_ukv"].reshape(depth, MLA_KV_LORA, MLA_HEADS, MLA_NOPE + MLA_V)
    order = np.array([0, 4, 1, 5, 2, 6, 3, 7])
    w_o_a = p["w_o_a"].reshape(depth, SWA_HEADS, SWA_HEAD_DIM, D_MODEL)[:, order].reshape(depth, SWA_Q_COLS,
                                                                                          D_MODEL)
    w_r = jnp.concatenate([p["w_rg"], p["w_re"]], axis=-1)
    w_r = jnp.pad(w_r, ((0, 0), (0, 0), (0, LANES - w_r.shape[-1])))
    w_r_hi = w_r.astype(BF16)
    b_r = jnp.concatenate([p["b_rg"], p["b_re"]], axis=-1)
    b_r = jnp.pad(b_r, ((0, 0), (0, LANES - b_r.shape[-1])))
    k_gain = p["mla_k_norm"]
    row = lambda a: a.reshape(depth, 1, a.shape[-1]).astype(F32)
    bs_full = jnp.repeat(jnp.swapaxes(p["gm_b_s"], 1, 2), LANES, axis=-1)
    return {
        "g_mix": row(p["g_mix"]), "g_ffn": row(p["g_ffn"]),
        "w_a": w_a.astype(BF16), "w_g": w_in[..., _GATE0:].astype(BF16),
        "gqa": row(jnp.tile(p["swa_q_norm"], (1, SWA_HEADS))),
        "gka": row(jnp.tile(p["swa_k_norm"], (1, SWA_KV_HEADS))),
        "gcq": row(p["mla_cq_norm"]), "gckv": row(p["mla_ckv_norm"]),
        "w_uq": _pad_heads(p["mla_w_uq"], MLA_HEADS, MLA_QK).astype(BF16),
        "gq": row(_pad_heads(jnp.tile(p["mla_q_norm"], (1, MLA_HEADS)), MLA_HEADS, MLA_QK)),
        "w_kn": _pad_heads(ukv[..., :MLA_NOPE].reshape(depth, MLA_KV_LORA, -1), MLA_HEADS, MLA_NOPE).astype(BF16),
        "w_v": ukv[..., MLA_NOPE:].reshape(depth, MLA_KV_LORA, -1).astype(BF16),
        "gk1": row(_pad_heads(jnp.tile(k_gain[:, :MLA_NOPE], (1, MLA_HEADS)), MLA_HEADS, MLA_NOPE)),
        "gk2": row(jnp.pad(k_gain[:, MLA_NOPE:], ((0, 0), (MLA_NOPE, LANES - MLA_QK)))),
        "ln_g": row(p["gm_ln_g"]), "ln_b": row(p["gm_ln_b"]),
        "w_s": p["gm_w_s"].astype(BF16), "b_s": bs_full.astype(F32),
        "w_o_a": w_o_a.astype(BF16), "w_o_b": p["w_o_b"].astype(BF16), "w_o_c": p["w_o_c"].astype(BF16),
        "w_out": p["w_out"].astype(BF16),
        "w_r_hi": w_r_hi, "w_r_lo": (w_r - w_r_hi.astype(F32)).astype(BF16), "b_r": row(b_r),
        "w_gate": p["w_gate"].astype(BF16), "w_up": p["w_up"].astype(BF16), "w_down": p["w_down"].astype(BF16),
    }


def kernel(x_prompt, x_sample, cache_swa_k, cache_swa_v, cache_mla_ckv, cache_mla_krope, c, c_ctx, w_mod, b_mod, g_mix, g_ffn, w_in, swa_q_norm, swa_k_norm, swa_sink, mla_cq_norm, mla_ckv_norm, mla_w_uq, mla_w_ukv, mla_q_norm, mla_k_norm, gm_ln_g, gm_ln_b, gm_w_s, gm_b_s, w_o_a, w_o_b, w_o_c, w_out, w_rg, b_rg, w_re, b_re, w_gate, w_up, w_down):
    d = D_MODEL
    n_ctx_seq, ctx_len, _ = x_prompt.shape
    n_lat_seq, lat_len, _ = x_sample.shape
    depth = w_in.shape[0]
    past = cache_swa_k.shape[2]
    n_ctx = n_ctx_seq * ctx_len
    n_lat = n_lat_seq * lat_len
    t = TOKEN_TILE
    assert ctx_len % t == 0 and lat_len % MLA_Q_TILE == 0 and lat_len % GRID_W == 0 and n_lat_seq < 8
    ctx_tiles = n_ctx // t
    lat_tiles = n_lat // t
    moe_tile = next(m for m in MOE_TILE_CANDIDATES if n_ctx % m == 0 and lat_len % m == 0)

    params = dict(w_in=w_in, g_mix=g_mix, g_ffn=g_ffn, swa_q_norm=swa_q_norm, swa_k_norm=swa_k_norm,
                  mla_cq_norm=mla_cq_norm, mla_ckv_norm=mla_ckv_norm, mla_w_uq=mla_w_uq, mla_w_ukv=mla_w_ukv,
                  mla_q_norm=mla_q_norm, mla_k_norm=mla_k_norm, gm_ln_g=gm_ln_g, gm_ln_b=gm_ln_b,
                  gm_w_s=gm_w_s, gm_b_s=gm_b_s, w_o_a=w_o_a, w_o_b=w_o_b, w_o_c=w_o_c, w_out=w_out,
                  w_rg=w_rg, b_rg=b_rg, w_re=w_re, b_re=b_re, w_gate=w_gate, w_up=w_up, w_down=w_down)
    w_all = _prep_weights(params)
    consts = {
        "ones64": _block_ones(SWA_HEAD_DIM), "ones128": _block_ones(HEAD_PAD),
        "rot_a": _rotate_half_matrix(SWA_HEAD_DIM, 0, SWA_HEAD_DIM),
        "rot_b": _rotate_half_matrix(HEAD_PAD, MLA_NOPE, MLA_ROPE),
    }
    cos_a, sin_a = _rope_tables(lat_len, SWA_HEAD_DIM, 0, SWA_HEAD_DIM)
    cos_b, sin_b = _rope_tables(lat_len, HEAD_PAD, MLA_NOPE, MLA_ROPE)
    tables = {"cos_a": cos_a, "sin_a": sin_a, "cos_b": cos_b, "sin_b": sin_b}

    cond8 = jnp.zeros((8, d), F32).at[:n_lat_seq].set(c).at[n_lat_seq].set(c_ctx)
    mods = _adaln(cond8, w_mod, b_mod).reshape(depth, 8, 1, 6 * d)

    k0 = cache_swa_k[:, :, :, 0, :]
    k1 = cache_swa_k[:, :, :, 1, :]
    kd_ctx_all = jnp.concatenate([k0, k0, k1, k1], axis=-1).astype(BF16)
    v_ctx_all = cache_swa_v.reshape(n_lat_seq, depth, past, SWA_KV_COLS).astype(BF16)
    kr_hi = jnp.pad(cache_mla_krope, ((0, 0), (0, 0), (0, 0), (MLA_NOPE, LANES - MLA_QK)))
    kb_ctx_all, vb_ctx_all = _ctx_keys(cache_mla_ckv, kr_hi, w_all["w_kn"], w_all["w_v"], w_all["gk1"],
                                       w_all["gk2"], consts["ones128"])

    tiles_per_lat_seq = lat_len // t
    ctx_row = lambda i: n_lat_seq
    lat_row = lambda i: i // tiles_per_lat_seq
    all_row = lambda i: jnp.where(i < ctx_tiles, n_lat_seq, (i - ctx_tiles) // tiles_per_lat_seq)
    moe_ctx_tiles = n_ctx // moe_tile
    moe_row = lambda i: jnp.where(i < moe_ctx_tiles, n_lat_seq, (i - moe_ctx_tiles) // (lat_len // moe_tile))

    x_all = jnp.concatenate([x_prompt.reshape(n_ctx, d), x_sample.reshape(n_lat, d)], axis=0)
    caches = [[], [], [], []]
    for l in range(depth):
        wl = {k: v[l] for k, v in w_all.items()}
        mod_l = mods[l]
        sink = swa_sink[l].reshape(SWA_HEADS)
        (qa_c, kd_c, va_c, qb_c, kb_c, vb_c, oc_c, ka_f, va_f, ckv_f, kr_f) = _token_stage(
            False, x_all, 0, ctx_tiles, mod_l, ctx_row, wl, consts, None)
        (qa_s, kd_s, va_s, qb_s, kb_s, vb_s, oc_s) = _token_stage(
            True, x_all, ctx_tiles, lat_tiles, mod_l, lat_row, wl, consts, tables)
        for dst, val in zip(caches, (ka_f, va_f, ckv_f, kr_f)):
            dst.append(val)
        oa_c, ob_c = _ctx_attention(sink, qa_c, kd_c, va_c, qb_c, kb_c, vb_c, n_ctx_seq, ctx_len)
        oa_s = _window_attention(sink, qa_s, kd_s, va_s, kd_ctx_all[:, l], v_ctx_all[:, l], n_lat_seq, lat_len)
        ob_s = _latent_mla(qb_s, kb_s, vb_s, kb_ctx_all[l], vb_ctx_all[l], n_lat_seq, lat_len)
        oa = jnp.concatenate([oa_c, oa_s], axis=0)
        ob = jnp.concatenate([ob_c, ob_s], axis=0)
        oc = jnp.concatenate([oc_c, oc_s], axis=0)
        x1, h2, comb = _finish(x_all, oa, ob, oc, mod_l, all_row, wl)
        x_all = _moe(h2, comb, x1, mod_l, moe_row, wl, moe_tile)

    y_prompt = x_all[:n_ctx].reshape(n_ctx_seq, ctx_len, d)
    y_sample = x_all[n_ctx:].reshape(n_lat_seq, lat_len, d)
    stack = lambda vals, tail: jnp.stack([v.reshape((n_ctx_seq, ctx_len) + tail) for v in vals], axis=1)
    return (y_prompt, y_sample,
            stack(caches[0], (SWA_KV_HEADS, SWA_HEAD_DIM)), stack(caches[1], (SWA_KV_HEADS, SWA_HEAD_DIM)),
            stack(caches[2], (MLA_KV_LORA,)), stack(caches[3], (MLA_ROPE,)))
```

```python
import functools

import numpy as np
import jax
import jax.numpy as jnp
from jax import lax
from jax.experimental import pallas as pl
from jax.experimental.pallas import tpu as pltpu

F32 = jnp.float32
BF16 = jnp.bfloat16

D_MODEL = 1024
GRID_W = 64
ROPE_THETA = 10000.0
EPS = 1e-6
NEG_INF = -1e30

SWA_HEADS = 8
SWA_KV_HEADS = 2
SWA_GROUP = SWA_HEADS // SWA_KV_HEADS
SWA_HEAD_DIM = 64
SWA_WINDOW = 128
SWA_Q_COLS = SWA_HEADS * SWA_HEAD_DIM
SWA_KV_COLS = SWA_KV_HEADS * SWA_HEAD_DIM

MLA_HEADS = 8
MLA_Q_LORA = 256
MLA_KV_LORA = 128
MLA_NOPE = 64
MLA_ROPE = 32
MLA_V = 64
MLA_QK = MLA_NOPE + MLA_ROPE

GM_CHUNK = 128
GM_GROUPS = 4
GM_WIDTH = 512

N_EXPERT_GROUPS = 4
EXPERTS_PER_GROUP = 4
N_EXPERTS = N_EXPERT_GROUPS * EXPERTS_PER_GROUP
EXPERT_FF = 512

LANES = 128
MXU_DIM = 256
VMEM_LIMIT_BYTES = 56 * 1024 * 1024

HEAD_PAD = LANES
MLA_PAD_COLS = MLA_HEADS * HEAD_PAD

TOKEN_TILE = 256
MLA_Q_TILE = 256
MLA_KEY_CHUNK = 512
MLA_SCORE_LOOKAHEAD = 2
LOG2_E = float(np.log2(np.e))
MOE_TILE_CANDIDATES = (1024, 512, 256)
MOE_CHUNK = 256

_QA0, _KA0, _VA0, _CQ0, _CKV0, _KR0, _U0, _V0, _ZCOLS = 0, 512, 640, 768, 1024, 1152, 1280, 1792, 2304
_GATE0 = SWA_Q_COLS + 2 * SWA_KV_COLS + MLA_Q_LORA + MLA_KV_LORA + MLA_ROPE + 2 * GM_WIDTH

ROUTER_LANE0 = N_EXPERT_GROUPS


def _cparams(*sem):
    return pltpu.CompilerParams(dimension_semantics=sem, vmem_limit_bytes=VMEM_LIMIT_BYTES)


def _nt_dot(a, b):
    return lax.dot_general(a, b, (((1,), (1,)), ((), ())), preferred_element_type=F32)


def _dot(a, b):
    return jnp.dot(a, b, preferred_element_type=F32)


def _gelu_tanh(x):
    return 0.5 * x * (1.0 + jnp.tanh(np.sqrt(2.0 / np.pi) * (x + 0.044715 * (x * x * x))))


def _group_sumsq(v, ones_ref):
    width = ones_ref.shape[0]
    parts = []
    for c in range(v.shape[1] // width):
        blk = v[:, c * width:(c + 1) * width]
        parts.append(_dot((blk * blk).astype(BF16), ones_ref[...]))
    return parts[0] if len(parts) == 1 else jnp.concatenate(parts, axis=1)


def _rope(v, cos, sin, rot_ref):
    width = rot_ref.shape[0]
    parts = []
    for c in range(v.shape[1] // width):
        blk = v[:, c * width:(c + 1) * width]
        rot = _dot(blk.astype(BF16), rot_ref[...])
        for s in range(width // LANES):
            sl = slice(s * LANES, (s + 1) * LANES)
            parts.append(blk[:, sl] * cos + rot[:, sl] * sin)
    return parts[0] if len(parts) == 1 else jnp.concatenate(parts, axis=1)


def _modulated_norm(x, gain, shift, scale):
    ms = jnp.mean(x * x, axis=-1, keepdims=True)
    h = x * lax.rsqrt(ms + EPS) * gain
    return h * (1.0 + scale) + shift


def _adaln_kernel(cond_ref, w_ref, b_ref, o_ref):
    a = cond_ref[...]
    a = a * jax.nn.sigmoid(a)
    o_ref[...] = _dot(a.astype(BF16), w_ref[...].astype(BF16)) + b_ref[...]


def _adaln(cond8, w_mod, b_mod):
    depth, d, n = w_mod.shape
    tn = 1536
    return pl.pallas_call(
        _adaln_kernel,
        grid=(depth, n // tn),
        in_specs=[
            pl.BlockSpec((8, d), lambda l, j: (0, 0)),
            pl.BlockSpec((None, d, tn), lambda l, j: (l, 0, j)),
            pl.BlockSpec((None, 1, tn), lambda l, j: (l, 0, j)),
        ],
        out_specs=pl.BlockSpec((None, 8, tn), lambda l, j: (l, 0, j)),
        out_shape=jax.ShapeDtypeStruct((depth, 8, n), F32),
        compiler_params=_cparams("parallel", "parallel"),
        name="adaln",
    )(cond8, w_mod, b_mod.reshape(depth, 1, n))


def _mla_keys_values(ckv_n, kr_hi, wkn_ref, wv_ref, gk1_ref, gk2_ref, ones128_ref, rope_args):
    cb = ckv_n.astype(BF16)
    nope = _dot(cb, wkn_ref[...])
    vals = _dot(cb, wv_ref[...])
    kr_ss = jnp.sum(kr_hi * kr_hi, axis=-1, keepdims=True)
    inv = lax.rsqrt((_group_sumsq(nope, ones128_ref) + kr_ss) * (1.0 / MLA_QK) + EPS)
    krg = kr_hi * gk2_ref[...]
    if rope_args is not None:
        cos, sin, rot_ref = rope_args
        krg = _rope(krg, cos, sin, rot_ref)
    gk1 = gk1_ref[...]
    parts = []
    for h in range(MLA_HEADS):
        sl = slice(h * HEAD_PAD, (h + 1) * HEAD_PAD)
        parts.append((nope[:, sl] * gk1[:, sl] + krg) * inv[:, sl])
    return jnp.concatenate(parts, axis=1), vals


def _values_with_sum_rows(vals_pad, vone_ref):
    return (vals_pad + vone_ref[...]).T.astype(BF16)


def _ctx_keys_kernel(ckv_ref, kr_ref, wkn_ref, wv_ref, gk1_ref, gk2_ref, vone_ref, ones128_ref, k_ref, vt_ref):
    keys, vals = _mla_keys_values(ckv_ref[...], kr_ref[...], wkn_ref, wv_ref, gk1_ref, gk2_ref,
                                  ones128_ref, None)
    k_ref[...] = keys.astype(BF16)
    vt_ref[...] = _values_with_sum_rows(vals, vone_ref)


def _ctx_keys(cache_ckv, cache_kr_hi, wkn, wvp, gk1, gk2, vone, ones128):
    nb, depth, past, _ = cache_ckv.shape
    cache_map = lambda l, b: (b, l, 0, 0)
    w_map = lambda l, b: (l, 0, 0)
    return pl.pallas_call(
        _ctx_keys_kernel,
        grid=(depth, nb),
        in_specs=[
            pl.BlockSpec((None, None, past, MLA_KV_LORA), cache_map),
            pl.BlockSpec((None, None, past, LANES), cache_map),
            pl.BlockSpec((None, MLA_KV_LORA, MLA_PAD_COLS), w_map),
            pl.BlockSpec((None, MLA_KV_LORA, MLA_PAD_COLS), w_map),
            pl.BlockSpec((None, 1, MLA_PAD_COLS), w_map),
            pl.BlockSpec((None, 1, LANES), w_map),
            pl.BlockSpec((1, MLA_PAD_COLS), lambda l, b: (0, 0)),
            pl.BlockSpec((MXU_DIM, MXU_DIM), lambda l, b: (0, 0)),
        ],
        out_specs=[
            pl.BlockSpec((None, None, past, MLA_PAD_COLS), lambda l, b: (l, b, 0, 0)),
            pl.BlockSpec((None, None, MLA_PAD_COLS, past), lambda l, b: (l, b, 0, 0)),
        ],
        out_shape=[
            jax.ShapeDtypeStruct((depth, nb, past, MLA_PAD_COLS), BF16),
            jax.ShapeDtypeStruct((depth, nb, MLA_PAD_COLS, past), BF16),
        ],
        compiler_params=_cparams("parallel", "parallel"),
        name="ctx_keys",
    )(cache_ckv, cache_kr_hi, wkn, wvp, gk1, gk2, vone, ones128)


def _token_kernel(latent, x_ref, mod_ref, gmix_ref, wa_ref, gqa_ref, gka_ref, gcq_ref, wuq_ref, gq_ref,
                  gckv_ref, wkn_ref, wv_ref, gk1_ref, gk2_ref, lng_ref, lnb_ref, ws_ref, bs_ref,
                  ones64_ref, ones128_ref, *rest):
    if latent:
        (cosa_ref, sina_ref, cosb_ref, sinb_ref, rota_ref, rotb_ref, vone_ref,
         qa_o, kd_o, va_o, qb_o, kb_o, vb_o, oc_o) = rest
    else:
        (qa_o, kd_o, va_o, qb_o, kb_o, vb_o, oc_o, kac_o, vac_o, ckvc_o, krc_o) = rest
    d = D_MODEL
    rows = x_ref.shape[0]
    h = _modulated_norm(x_ref[...], gmix_ref[...], mod_ref[:, 0:d], mod_ref[:, d:2 * d])
    z = _dot(h.astype(BF16), wa_ref[...])

    qa = z[:, _QA0:_KA0]
    qa = qa * lax.rsqrt(_group_sumsq(qa, ones64_ref) * (1.0 / SWA_HEAD_DIM) + EPS) * gqa_ref[...]
    ka = z[:, _KA0:_VA0]
    ka_ss = _dot((ka * ka).astype(BF16), ones64_ref[0:LANES, 0:LANES])
    ka = ka * lax.rsqrt(ka_ss * (1.0 / SWA_HEAD_DIM) + EPS) * gka_ref[...]
    va = z[:, _VA0:_CQ0]
    if latent:
        qa = _rope(qa, cosa_ref[...], sina_ref[...], rota_ref)
        ka_r = _rope(ka, cosa_ref[...], sina_ref[...], rota_ref.at[0:LANES, 0:LANES])
    else:
        kac_o[...] = ka
        vac_o[...] = va
        ka_r = ka
    swapped = pltpu.roll(ka_r, SWA_HEAD_DIM, 1)
    low = lax.broadcasted_iota(jnp.int32, (rows, LANES), 1) < SWA_HEAD_DIM
    qa_o[...] = qa.astype(BF16)
    kd_o[:, 0:LANES] = jnp.where(low, ka_r, swapped).astype(BF16)
    kd_o[:, LANES:2 * LANES] = jnp.where(low, swapped, ka_r).astype(BF16)
    va_o[...] = va.astype(BF16)

    cq = z[:, _CQ0:_CKV0]
    cq = cq * lax.rsqrt(jnp.mean(cq * cq, axis=-1, keepdims=True) + EPS) * gcq_ref[...]
    qb = _dot(cq.astype(BF16), wuq_ref[...])
    qb = qb * lax.rsqrt(_group_sumsq(qb, ones128_ref) * (1.0 / MLA_QK) + EPS) * gq_ref[...]
    if latent:
        qb = _rope(qb, cosb_ref[...], sinb_ref[...], rotb_ref)
    qb_o[...] = qb.astype(BF16)

    ckv = z[:, _CKV0:_KR0]
    ckv = ckv * lax.rsqrt(jnp.mean(ckv * ckv, axis=-1, keepdims=True) + EPS) * gckv_ref[...]
    krb = z[:, _KR0:_U0]
    lane = lax.broadcasted_iota(jnp.int32, (rows, LANES), 1)
    kr_hi = jnp.where(lane >= MLA_NOPE, krb, 0.0)
    rope_args = (cosb_ref[...], sinb_ref[...], rotb_ref.at[0:LANES, 0:LANES]) if latent else None
    kb, vb = _mla_keys_values(ckv, kr_hi, wkn_ref, wv_ref, gk1_ref, gk2_ref, ones128_ref, rope_args)
    kb_o[...] = kb.astype(BF16)
    if latent:
        vb_o[...] = _values_with_sum_rows(vb, vone_ref)
    else:
        vb_o[...] = vb.astype(BF16)
    if not latent:
        ckvc_o[...] = ckv
        krc_o[...] = krb[:, 0:MLA_ROPE]

    gu = _gelu_tanh(z[:, _U0:_V0])
    gv = _gelu_tanh(z[:, _V0:_ZCOLS])
    gc = gv - jnp.mean(gv, axis=-1, keepdims=True)
    var = jnp.mean(gc * gc, axis=-1, keepdims=True)
    vg = (gc * lax.rsqrt(var + EPS) * lng_ref[...] + lnb_ref[...]).astype(BF16)
    n_chunks = rows // GM_CHUNK
    for g in range(GM_GROUPS):
        gl = slice(g * LANES, (g + 1) * LANES)
        rhs = jnp.concatenate([vg[c * GM_CHUNK:(c + 1) * GM_CHUNK, gl] for c in range(n_chunks)], axis=1)
        mixed = _dot(ws_ref[g], rhs)
        for c in range(n_chunks):
            rs = slice(c * GM_CHUNK, (c + 1) * GM_CHUNK)
            oc_o[rs, gl] = (gu[rs, gl] * (mixed[:, c * LANES:(c + 1) * LANES] + bs_ref[:, gl])).astype(BF16)


def _token_stage(latent, x, n_tiles, mod_l, mod_row, wl, consts, tables):
    t = TOKEN_TILE
    d = D_MODEL
    n = n_tiles * t
    full = lambda a: pl.BlockSpec(a.shape, lambda i: (0,) * a.ndim)
    weights = [wl["g_mix"], wl["w_a"], wl["gqa"], wl["gka"], wl["gcq"], wl["w_uq"], wl["gq"], wl["gckv"],
               wl["w_kn"], wl["w_vp"] if latent else wl["w_v"], wl["gk1"], wl["gk2"], wl["ln_g"], wl["ln_b"], wl["w_s"], wl["b_s"],
               consts["ones64"], consts["ones128"]]
    in_specs = [
        pl.BlockSpec((t, d), lambda i: (i, 0)),
        pl.BlockSpec((None, 1, 6 * d), lambda i: (mod_row(i), 0, 0)),
    ] + [full(a) for a in weights]
    args = [x, mod_l] + weights
    row = lambda w: pl.BlockSpec((t, w), lambda i: (i, 0))
    out_widths = [SWA_Q_COLS, 2 * LANES, LANES, MLA_PAD_COLS, MLA_PAD_COLS, MLA_HEADS * MLA_V, GM_WIDTH]
    out_specs = [row(w) for w in out_widths]
    out_shape = [jax.ShapeDtypeStruct((n, w), BF16) for w in out_widths]
    if latent:
        tiles_per_seq = tables["cos_a"].shape[0] // t
        tab = lambda: pl.BlockSpec((t, LANES), lambda i: (i % tiles_per_seq, 0))
        in_specs += [tab(), tab(), tab(), tab(), full(consts["rot_a"]), full(consts["rot_b"]),
                     full(consts["v_ones"])]
        args += [tables["cos_a"], tables["sin_a"], tables["cos_b"], tables["sin_b"],
                 consts["rot_a"], consts["rot_b"], consts["v_ones"]]
        out_specs[5] = pl.BlockSpec((MLA_PAD_COLS, t), lambda i: (0, i))
        out_shape[5] = jax.ShapeDtypeStruct((MLA_PAD_COLS, n), BF16)
    else:
        cache_widths = [SWA_KV_COLS, SWA_KV_COLS, MLA_KV_LORA, MLA_ROPE]
        out_specs += [row(w) for w in cache_widths]
        out_shape += [jax.ShapeDtypeStruct((n, w), F32) for w in cache_widths]
    return pl.pallas_call(
        functools.partial(_token_kernel, latent),
        grid=(n_tiles,),
        in_specs=in_specs,
        out_specs=out_specs,
        out_shape=out_shape,
        compiler_params=_cparams("parallel"),
        name="token_stage_latent" if latent else "token_stage_ctx",
    )(*args)


def _softmax_pv(scores, values, extra=None):
    m = jnp.max(scores[0], axis=-1, keepdims=True)
    for s in scores[1:]:
        m = jnp.maximum(m, jnp.max(s, axis=-1, keepdims=True))
    if extra is not None:
        m = jnp.maximum(m, extra)
    den = jnp.exp(extra - m) if extra is not None else 0.0
    acc = None
    for s, v in zip(scores, values):
        p = jnp.exp(s - m)
        den = den + jnp.sum(p, axis=-1, keepdims=True)
        pv = _dot(p.astype(BF16), v)
        acc = pv if acc is None else acc + pv
    return acc / den


def _ctx_attn_kernel(sink_ref, qa_ref, kd_ref, va_ref, qb_ref, kb_ref, vb_ref, oa_ref, ob_ref):
    rows = qa_ref.shape[0]
    low = lax.broadcasted_iota(jnp.int32, (rows, LANES), 1) < SWA_HEAD_DIM
    va = va_ref[...]
    res = []
    for hd in range(SWA_HEADS):
        qblk = qa_ref[:, (hd // 2) * LANES:(hd // 2 + 1) * LANES]
        keep = low if hd % 2 == 0 else jnp.logical_not(low)
        qm = jnp.where(keep, qblk, jnp.zeros_like(qblk))
        kvh = hd // SWA_GROUP
        s = _nt_dot(qm, kd_ref[:, kvh * LANES:(kvh + 1) * LANES]) * (SWA_HEAD_DIM ** -0.5)
        res.append(_softmax_pv([s], [va], sink_ref[hd]))
    for j in range(SWA_GROUP):
        oa_ref[:, j * LANES:(j + 1) * LANES] = jnp.where(low, res[j], res[SWA_GROUP + j]).astype(BF16)
    res = []
    for h in range(MLA_HEADS):
        sl = slice(h * HEAD_PAD, (h + 1) * HEAD_PAD)
        s = _nt_dot(qb_ref[:, sl], kb_ref[:, sl]) * (MLA_QK ** -0.5)
        res.append(_softmax_pv([s], [vb_ref[:, (h // 2) * LANES:(h // 2 + 1) * LANES]]))
    for j in range(MLA_HEADS // 2):
        ob_ref[:, j * LANES:(j + 1) * LANES] = jnp.where(low, res[2 * j], res[2 * j + 1]).astype(BF16)


def _ctx_attention(sink, qa, kd, va, qb, kb, vb, n_seq, seq):
    row = lambda w: pl.BlockSpec((seq, w), lambda b: (b, 0))
    widths = [SWA_Q_COLS, 2 * LANES, LANES, MLA_PAD_COLS, MLA_PAD_COLS, MLA_HEADS * MLA_V]
    return pl.pallas_call(
        _ctx_attn_kernel,
        grid=(n_seq,),
        in_specs=[pl.BlockSpec(memory_space=pltpu.SMEM)] + [row(w) for w in widths],
        out_specs=[row(SWA_Q_COLS), row(MLA_HEADS * MLA_V)],
        out_shape=[jax.ShapeDtypeStruct((n_seq * seq, SWA_Q_COLS), BF16),
                   jax.ShapeDtypeStruct((n_seq * seq, MLA_HEADS * MLA_V), BF16)],
        compiler_params=_cparams("parallel"),
        name="ctx_attention",
    )(sink, qa, kd, va, qb, kb, vb)


def _window_attn_kernel(sink_ref, q_ref, kp_ref, kc_ref, kn_ref, vp_ref, vc_ref, vn_ref, kctx_ref, vctx_ref,
                        o_ref):
    w = SWA_WINDOW
    i = pl.program_id(1)
    nb = pl.num_programs(1)
    low = lax.broadcasted_iota(jnp.int32, (w, LANES), 1) < SWA_HEAD_DIM
    rows = SWA_GROUP * w
    r_io = lax.broadcasted_iota(jnp.int32, (rows, 3 * w), 0)
    c_io = lax.broadcasted_iota(jnp.int32, (rows, 3 * w), 1)
    rel = c_io - (r_io & (w - 1))
    first_col = jnp.where(i == 0, w, 0)
    end_col = jnp.where(i == nb - 1, 2 * w, 3 * w)
    valid = (rel >= 0) & (rel <= 2 * w) & (c_io >= first_col) & (c_io < end_col)
    g_io = lax.broadcasted_iota(jnp.int32, (rows, 1), 0) >> (w.bit_length() - 1)
    v_loc = jnp.concatenate([vp_ref[...], vc_ref[...], vn_ref[...]], axis=0)
    v_ctx = vctx_ref[...]
    res = []
    for kvh in range(SWA_KV_HEADS):
        qs = []
        sink = jnp.zeros((rows, 1), F32)
        for g in range(SWA_GROUP):
            hd = kvh * SWA_GROUP + g
            qblk = q_ref[:, (hd // 2) * LANES:(hd // 2 + 1) * LANES]
            keep = low if hd % 2 == 0 else jnp.logical_not(low)
            qs.append(jnp.where(keep, qblk, jnp.zeros_like(qblk)))
            sink = jnp.where(g_io == g, sink_ref[hd], sink)
        qs = jnp.concatenate(qs, axis=0)
        ks = slice(kvh * LANES, (kvh + 1) * LANES)
        k_loc = jnp.concatenate([kp_ref[:, ks], kc_ref[:, ks], kn_ref[:, ks]], axis=0)
        s_ctx = _nt_dot(qs, kctx_ref[:, ks]) * (SWA_HEAD_DIM ** -0.5)
        s_loc = _nt_dot(qs, k_loc) * (SWA_HEAD_DIM ** -0.5)
        s_loc = jnp.where(valid, s_loc, NEG_INF)
        res.append(_softmax_pv([s_ctx, s_loc], [v_ctx, v_loc], sink))
    for j in range(SWA_GROUP):
        rs = slice(j * w, (j + 1) * w)
        o_ref[:, j * LANES:(j + 1) * LANES] = jnp.where(low, res[0][rs], res[1][rs]).astype(BF16)


def _window_attention(sink, qa, kd, va, kd_ctx, v_ctx, n_seq, seq):
    w = SWA_WINDOW
    nb = seq // w
    past = kd_ctx.shape[1]
    cur = lambda b, i: (b * nb + i, 0)
    prev = lambda b, i: (b * nb + jnp.maximum(i - 1, 0), 0)
    nxt = lambda b, i: (b * nb + jnp.minimum(i + 1, nb - 1), 0)
    return pl.pallas_call(
        _window_attn_kernel,
        grid=(n_seq, nb),
        in_specs=[
            pl.BlockSpec(memory_space=pltpu.SMEM),
            pl.BlockSpec((w, SWA_Q_COLS), cur),
            pl.BlockSpec((w, 2 * LANES), prev),
            pl.BlockSpec((w, 2 * LANES), cur),
            pl.BlockSpec((w, 2 * LANES), nxt),
            pl.BlockSpec((w, LANES), prev),
            pl.BlockSpec((w, LANES), cur),
            pl.BlockSpec((w, LANES), nxt),
            pl.BlockSpec((None, past, 2 * LANES), lambda b, i: (b, 0, 0)),
            pl.BlockSpec((None, past, LANES), lambda b, i: (b, 0, 0)),
        ],
        out_specs=pl.BlockSpec((w, SWA_Q_COLS), cur),
        out_shape=jax.ShapeDtypeStruct((n_seq * seq, SWA_Q_COLS), BF16),
        compiler_params=_cparams("parallel", "parallel"),
        name="window_attention",
    )(sink, qa, kd, kd, kd, va, va, va, kd_ctx, v_ctx)


def _latent_mla_kernel(q_ref, k_ref, vt_ref, kctx_ref, vtctx_ref, o_ref):
    kc = MLA_KEY_CHUNK
    c = (MLA_QK ** -0.5) * LOG2_E
    chunks = ([(kctx_ref, vtctx_ref, j) for j in range(kctx_ref.shape[0] // kc)]
              + [(k_ref, vt_ref, j) for j in range(k_ref.shape[0] // kc)])
    heads = [slice(e * HEAD_PAD, (e + 1) * HEAD_PAD) for e in range(2)]

    def scores(n):
        keys_ref, _, j = chunks[n]
        return [_nt_dot(keys_ref[j * kc:(j + 1) * kc, sl], q_ref[:, sl]) for sl in heads]

    m = [None, None]
    acc = [None, None]
    pending = [scores(n) for n in range(min(MLA_SCORE_LOOKAHEAD, len(chunks)))]
    for n, (_, vals_ref, j) in enumerate(chunks):
        if n + MLA_SCORE_LOOKAHEAD < len(chunks):
            pending.append(scores(n + MLA_SCORE_LOOKAHEAD))
        st = pending.pop(0)
        for e, sl in enumerate(heads):
            cmax = jnp.max(st[e], axis=0, keepdims=True)
            vals = vals_ref[sl, j * kc:(j + 1) * kc]
            if n == 0:
                m[e] = cmax
                acc[e] = _dot(vals, jnp.exp2((st[e] - cmax) * c).astype(BF16))
            else:
                m_new = jnp.maximum(m[e], cmax)
                p = jnp.exp2((st[e] - m_new) * c).astype(BF16)
                acc[e] = acc[e] * jnp.exp2((m[e] - m_new) * c) + _dot(vals, p)
                m[e] = m_new
    outs = [a[0:MLA_V] / a[MLA_V:MLA_V + 1] for a in acc]
    o_ref[...] = jnp.concatenate(outs, axis=0).T.astype(BF16)


def _latent_mla(qb, kb, vb, k_ctx, v_ctx, n_seq, seq):
    tq = MLA_Q_TILE
    nq = seq // tq
    past = k_ctx.shape[1]
    pair = 2 * HEAD_PAD
    assert past % MLA_KEY_CHUNK == 0 and seq % MLA_KEY_CHUNK == 0
    return pl.pallas_call(
        _latent_mla_kernel,
        grid=(n_seq, MLA_HEADS // 2, nq),
        in_specs=[
            pl.BlockSpec((tq, pair), lambda b, p, i: (b * nq + i, p)),
            pl.BlockSpec((seq, pair), lambda b, p, i: (b, p)),
            pl.BlockSpec((pair, seq), lambda b, p, i: (p, b)),
            pl.BlockSpec((None, past, pair), lambda b, p, i: (b, 0, p)),
            pl.BlockSpec((None, pair, past), lambda b, p, i: (b, p, 0)),
        ],
        out_specs=pl.BlockSpec((tq, LANES), lambda b, p, i: (b * nq + i, p)),
        out_shape=jax.ShapeDtypeStruct((n_seq * seq, MLA_HEADS * MLA_V), BF16),
        compiler_params=_cparams("parallel", "parallel", "parallel"),
        name="latent_mla",
    )(qb, kb, vb, k_ctx, v_ctx)


def _route(logits):
    lane = lax.broadcasted_iota(jnp.int32, logits.shape, 1).astype(F32)
    big = float(4 * LANES)
    gmask = lane < N_EXPERT_GROUPS
    gmax = jnp.max(jnp.where(gmask, logits, -jnp.inf), axis=-1, keepdims=True)
    gsum = jnp.sum(jnp.where(gmask, jnp.exp(logits - gmax), 0.0), axis=-1, keepdims=True)
    g_w = 1.0 / gsum
    g_idx = jnp.min(jnp.where(gmask & (logits == gmax), lane, big), axis=-1, keepdims=True)
    first = ROUTER_LANE0 + EXPERTS_PER_GROUP * g_idx
    emask = (lane >= first) & (lane < first + EXPERTS_PER_GROUP)
    el = jnp.where(emask, logits, -jnp.inf)
    m1 = jnp.max(el, axis=-1, keepdims=True)
    i1 = jnp.min(jnp.where(emask & (el == m1), lane, big), axis=-1, keepdims=True)
    el2 = jnp.where(lane == i1, -jnp.inf, el)
    m2 = jnp.max(el2, axis=-1, keepdims=True)
    i2 = jnp.min(jnp.where(emask & (el2 == m2), lane, big), axis=-1, keepdims=True)
    esum = jnp.sum(jnp.where(emask, jnp.exp(logits - m1), 0.0), axis=-1, keepdims=True)
    p1 = 1.0 / esum
    p2 = jnp.exp(m2 - m1) / esum
    tot = p1 + p2
    combine = jnp.where(lane == i1, g_w * (p1 / tot), 0.0) + jnp.where(lane == i2, g_w * (p2 / tot), 0.0)
    return jnp.where(lane == g_idx, 1.0, combine)


def _finish_kernel(x_ref, oa_ref, ob_ref, oc_ref, mod_ref, gmix_ref, gffn_ref, wg_ref, woa_ref, wob_ref,
                   woc_ref, wout_ref, wrh_ref, wrl_ref, br_ref, x1_ref, h2_ref, comb_ref):
    d = D_MODEL
    x = x_ref[...]
    h = _modulated_norm(x, gmix_ref[...], mod_ref[:, 0:d], mod_ref[:, d:2 * d]).astype(BF16)
    gates = _dot(h, wg_ref[...])
    merged = (jax.nn.sigmoid(gates[:, 0:d]) * _dot(oa_ref[...], woa_ref[...])
              + jax.nn.sigmoid(gates[:, d:2 * d]) * _dot(ob_ref[...], wob_ref[...])
              + jax.nn.sigmoid(gates[:, 2 * d:3 * d]) * _dot(oc_ref[...], woc_ref[...]))
    x1 = x + mod_ref[:, 2 * d:3 * d] * _dot(merged.astype(BF16), wout_ref[...])
    x1_ref[...] = x1
    h2 = _modulated_norm(x1, gffn_ref[...], mod_ref[:, 3 * d:4 * d], mod_ref[:, 4 * d:5 * d])
    h2_hi = h2.astype(BF16)
    h2_lo = (h2 - h2_hi.astype(F32)).astype(BF16)
    h2_ref[...] = h2_hi
    logits = (_dot(h2_hi, wrh_ref[...]) + _dot(h2_lo, wrh_ref[...]) + _dot(h2_hi, wrl_ref[...])) + br_ref[...]
    comb_ref[...] = _route(logits)


def _finish(x_all, oa, ob, oc, mod_l, mod_row, wl):
    t = TOKEN_TILE
    d = D_MODEL
    n = x_all.shape[0]
    full = lambda a: pl.BlockSpec(a.shape, lambda i: (0,) * a.ndim)
    row = lambda w: pl.BlockSpec((t, w), lambda i: (i, 0))
    weights = [wl["g_mix"], wl["g_ffn"], wl["w_g"], wl["w_o_a"], wl["w_o_b"], wl["w_o_c"], wl["w_out"],
               wl["w_r_hi"], wl["w_r_lo"], wl["b_r"]]
    return pl.pallas_call(
        _finish_kernel,
        grid=(n // t,),
        in_specs=[row(d), row(SWA_Q_COLS), row(MLA_HEADS * MLA_V), row(GM_WIDTH),
                  pl.BlockSpec((None, 1, 6 * d), lambda i: (mod_row(i), 0, 0))] + [full(a) for a in weights],
        out_specs=[row(d), row(d), row(LANES)],
        out_shape=[jax.ShapeDtypeStruct((n, d), F32), jax.ShapeDtypeStruct((n, d), BF16),
                   jax.ShapeDtypeStruct((n, LANES), F32)],
        compiler_params=_cparams("parallel"),
        name="finish",
    )(x_all, oa, ob, oc, mod_l, *weights)


def _moe_kernel(h_ref, route_ref, x1_ref, mod_ref, tri_ref, wg_ref, wu_ref, wd_ref, o_ref,
                route_t, rank_rows, rank_cols, xg, yacc, wsel):
    d = D_MODEL
    t = h_ref.shape[0]
    ch = MOE_CHUNK
    e = pl.program_id(1)
    g = lax.shift_right_logical(e, 2)
    j = e & (EXPERTS_PER_GROUP - 1)

    @pl.when(e == 0)
    def _():
        rt = route_ref[...].T
        route_t[...] = rt
        rr = _dot(rt.astype(BF16), tri_ref[...])
        rank_rows[...] = rr
        rank_cols[...] = rr.T
        o_ref[...] = jnp.zeros_like(o_ref)

    ind_row = route_t[pl.ds(g, 1), :]
    n_rows = jnp.sum(ind_row).astype(jnp.int32)
    n_chunks = lax.shift_right_logical(n_rows + (ch - 1), ch.bit_length() - 1)

    def chunk_rows(c):
        return pl.ds(pl.multiple_of(c * ch, ch), ch)

    @pl.when(j == 0)
    def _():
        rank_row = rank_rows[pl.ds(g, 1), :]
        member = ind_row > 0.5
        r_io = lax.broadcasted_iota(jnp.int32, (ch, t), 0).astype(F32)
        lane = lax.broadcasted_iota(jnp.int32, (ch, LANES), 1)

        def gather(c, carry):
            sel = (rank_row == r_io + (c * ch).astype(F32)) & member
            rows = chunk_rows(c)
            xg[rows, :] = _dot(jnp.where(sel, 1.0, 0.0).astype(BF16), h_ref[...]).astype(BF16)
            wt = jnp.zeros((ch, LANES), F32)
            for jj in range(EXPERTS_PER_GROUP):
                w_row = route_t[pl.ds(ROUTER_LANE0 + EXPERTS_PER_GROUP * g + jj, 1), :]
                w_col = jnp.sum(jnp.where(sel, w_row, 0.0), axis=-1, keepdims=True)
                wt = jnp.where(lane == jj, w_col, wt)
            wsel[rows, :] = wt
            yacc[rows, :] = jnp.zeros((ch, d), F32)
            return carry

        lax.fori_loop(0, n_chunks, gather, 0)

    def ffn(c, carry):
        rows = chunk_rows(c)
        x = xg[rows, :]
        a = _dot(x, wg_ref[...])
        hid = (a * jax.nn.sigmoid(a)) * _dot(x, wu_ref[...])
        y = _dot(hid.astype(BF16), wd_ref[...])
        lane = lax.broadcasted_iota(jnp.int32, (ch, LANES), 1)
        w_col = jnp.sum(jnp.where(lane == j, wsel[rows, :], 0.0), axis=-1, keepdims=True)
        yacc[rows, :] += w_col * y
        return carry

    lax.fori_loop(0, n_chunks, ffn, 0)

    @pl.when(j == EXPERTS_PER_GROUP - 1)
    def _():
        lane_t = lax.broadcasted_iota(jnp.int32, (t, LANES), 1)
        member = jnp.sum(jnp.where(lane_t == g, route_ref[...], 0.0), axis=-1, keepdims=True) > 0.5
        rank_col = jnp.sum(jnp.where(lane_t == g, rank_cols[...], 0.0), axis=-1, keepdims=True)
        c_io = lax.broadcasted_iota(jnp.int32, (t, ch), 1).astype(F32)

        def scatter(c, carry):
            sel_t = (rank_col == c_io + (c * ch).astype(F32)) & member
            o_ref[...] += _dot(jnp.where(sel_t, 1.0, 0.0).astype(BF16), yacc[chunk_rows(c), :].astype(BF16))
            return carry

        lax.fori_loop(0, n_chunks, scatter, 0)

    @pl.when(e == pl.num_programs(1) - 1)
    def _():
        o_ref[...] = x1_ref[...] + mod_ref[:, 5 * d:6 * d] * o_ref[...]


def _moe(h2, route, x1, mod_l, mod_row, wl, tri):
    d = D_MODEL
    n = h2.shape[0]
    tile = tri.shape[0]
    row = lambda w: pl.BlockSpec((tile, w), lambda i, e: (i, 0))
    return pl.pallas_call(
        _moe_kernel,
        grid=(n // tile, N_EXPERTS),
        in_specs=[row(d), row(LANES), row(d),
                  pl.BlockSpec((None, 1, 6 * d), lambda i, e: (mod_row(i), 0, 0)),
                  pl.BlockSpec((tile, tile), lambda i, e: (0, 0)),
                  pl.BlockSpec((None, d, EXPERT_FF), lambda i, e: (e, 0, 0)),
                  pl.BlockSpec((None, d, EXPERT_FF), lambda i, e: (e, 0, 0)),
                  pl.BlockSpec((None, EXPERT_FF, d), lambda i, e: (e, 0, 0))],
        out_specs=row(d),
        out_shape=jax.ShapeDtypeStruct((n, d), F32),
        scratch_shapes=[pltpu.VMEM((LANES, tile), F32), pltpu.VMEM((LANES, tile), F32),
                        pltpu.VMEM((tile, LANES), F32), pltpu.VMEM((tile, d), BF16),
                        pltpu.VMEM((tile, d), F32), pltpu.VMEM((tile, LANES), F32)],
        compiler_params=_cparams("parallel", "arbitrary"),
        name="moe_grouped",
    )(h2, route, x1, mod_l, tri, wl["w_gate"], wl["w_up"], wl["w_down"])


def _block_ones(group):
    idx = np.arange(MXU_DIM)
    return jnp.asarray((idx[:, None] // group) == (idx[None, :] // group), BF16)


def _rotate_half_matrix(block, rot_start, rot_dim):
    half = rot_dim // 4
    r = np.zeros((MXU_DIM, MXU_DIM), np.float32)
    for j in range(MXU_DIM):
        o = j % block - rot_start
        if 0 <= o < rot_dim:
            partner = j + half if (o % (2 * half)) < half else j - half
            r[partner, j] = 1.0
    return jnp.asarray(r, BF16)


def _rope_tables(n_tokens, block, rot_start, rot_dim):
    half = rot_dim // 4
    tok = np.arange(n_tokens)
    pos = np.stack([tok // GRID_W, tok % GRID_W], axis=1).astype(np.float32)
    inv_freq = (ROPE_THETA ** (-np.arange(half, dtype=np.float32) / half)).astype(np.float32)
    lane = np.arange(LANES)
    o = lane % block - rot_start
    rot = (o >= 0) & (o < rot_dim)
    oc = np.where(rot, o, 0)
    axis = oc // (2 * half)
    freq = inv_freq[oc % half]
    sign = np.where((oc % (2 * half)) < half, -1.0, 1.0).astype(np.float32)
    ang = jnp.asarray(pos[:, axis]) * jnp.asarray(freq)[None, :]
    rot_j = jnp.asarray(rot)[None, :]
    cos = jnp.where(rot_j, jnp.cos(ang), 1.0)
    sin = jnp.where(rot_j, jnp.sin(ang) * jnp.asarray(sign)[None, :], 0.0)
    return cos.astype(F32), sin.astype(F32)


def _pad_heads(w, heads, width):
    lead = w.shape[:-1]
    w = w.reshape(lead + (heads, width))
    w = jnp.pad(w, [(0, 0)] * len(lead) + [(0, 0), (0, HEAD_PAD - width)])
    return w.reshape(lead + (heads * HEAD_PAD,))


def _prep_weights(p):
    depth = p["w_in"].shape[0]
    w_in = p["w_in"]
    c = np.cumsum((0, SWA_Q_COLS, SWA_KV_COLS, SWA_KV_COLS, MLA_Q_LORA, MLA_KV_LORA, MLA_ROPE, GM_WIDTH,
                   GM_WIDTH))
    kr = w_in[..., c[5]:c[6]]
    zero = jnp.zeros_like(kr)
    w_a = jnp.concatenate([w_in[..., c[0]:c[5]], kr, zero, kr, zero, w_in[..., c[6]:c[8]]], axis=-1)
    ukv = p["mla_w_ukv"].reshape(depth, MLA_KV_LORA, MLA_HEADS, MLA_NOPE + MLA_V)
    order = np.array([0, 4, 1, 5, 2, 6, 3, 7])
    w_o_a = p["w_o_a"].reshape(depth, SWA_HEADS, SWA_HEAD_DIM, D_MODEL)[:, order].reshape(depth, SWA_Q_COLS,
                                                                                          D_MODEL)
    w_r = jnp.concatenate([p["w_rg"], p["w_re"]], axis=-1)
    w_r = jnp.pad(w_r, ((0, 0), (0, 0), (0, LANES - w_r.shape[-1])))
    w_r_hi = w_r.astype(BF16)
    b_r = jnp.concatenate([p["b_rg"], p["b_re"]], axis=-1)
    b_r = jnp.pad(b_r, ((0, 0), (0, LANES - b_r.shape[-1])))
    k_gain = p["mla_k_norm"]
    row = lambda a: a.reshape(depth, 1, a.shape[-1]).astype(F32)
    bs_full = jnp.repeat(jnp.swapaxes(p["gm_b_s"], 1, 2), LANES, axis=-1)
    return {
        "g_mix": row(p["g_mix"]), "g_ffn": row(p["g_ffn"]),
        "w_a": w_a.astype(BF16), "w_g": w_in[..., _GATE0:].astype(BF16),
        "gqa": row(jnp.tile(p["swa_q_norm"], (1, SWA_HEADS))),
        "gka": row(jnp.tile(p["swa_k_norm"], (1, SWA_KV_HEADS))),
        "gcq": row(p["mla_cq_norm"]), "gckv": row(p["mla_ckv_norm"]),
        "w_uq": _pad_heads(p["mla_w_uq"], MLA_HEADS, MLA_QK).astype(BF16),
        "gq": row(_pad_heads(jnp.tile(p["mla_q_norm"], (1, MLA_HEADS)), MLA_HEADS, MLA_QK)),
        "w_kn": _pad_heads(ukv[..., :MLA_NOPE].reshape(depth, MLA_KV_LORA, -1), MLA_HEADS, MLA_NOPE).astype(BF16),
        "w_v": ukv[..., MLA_NOPE:].reshape(depth, MLA_KV_LORA, -1).astype(BF16),
        "w_vp": _pad_heads(ukv[..., MLA_NOPE:].reshape(depth, MLA_KV_LORA, -1), MLA_HEADS, MLA_V).astype(BF16),
        "gk1": row(_pad_heads(jnp.tile(k_gain[:, :MLA_NOPE], (1, MLA_HEADS)), MLA_HEADS, MLA_NOPE)),
        "gk2": row(jnp.pad(k_gain[:, MLA_NOPE:], ((0, 0), (MLA_NOPE, LANES - MLA_QK)))),
        "ln_g": row(p["gm_ln_g"]), "ln_b": row(p["gm_ln_b"]),
        "w_s": p["gm_w_s"].astype(BF16), "b_s": bs_full.astype(F32),
        "w_o_a": w_o_a.astype(BF16), "w_o_b": p["w_o_b"].astype(BF16), "w_o_c": p["w_o_c"].astype(BF16),
        "w_out": p["w_out"].astype(BF16),
        "w_r_hi": w_r_hi, "w_r_lo": (w_r - w_r_hi.astype(F32)).astype(BF16), "b_r": row(b_r),
        "w_gate": p["w_gate"].astype(BF16), "w_up": p["w_up"].astype(BF16), "w_down": p["w_down"].astype(BF16),
    }


def kernel(x_prompt, x_sample, cache_swa_k, cache_swa_v, cache_mla_ckv, cache_mla_krope, c, c_ctx, w_mod, b_mod, g_mix, g_ffn, w_in, swa_q_norm, swa_k_norm, swa_sink, mla_cq_norm, mla_ckv_norm, mla_w_uq, mla_w_ukv, mla_q_norm, mla_k_norm, gm_ln_g, gm_ln_b, gm_w_s, gm_b_s, w_o_a, w_o_b, w_o_c, w_out, w_rg, b_rg, w_re, b_re, w_gate, w_up, w_down):
    d = D_MODEL
    n_ctx_seq, ctx_len, _ = x_prompt.shape
    n_lat_seq, lat_len, _ = x_sample.shape
    depth = w_in.shape[0]
    past = cache_swa_k.shape[2]
    n_ctx = n_ctx_seq * ctx_len
    n_lat = n_lat_seq * lat_len
    t = TOKEN_TILE
    assert ctx_len % t == 0 and lat_len % MLA_Q_TILE == 0 and lat_len % GRID_W == 0 and n_lat_seq < 8
    ctx_tiles = n_ctx // t
    lat_tiles = n_lat // t
    moe_tile = next(m for m in MOE_TILE_CANDIDATES if n_ctx % m == 0 and lat_len % m == 0)

    params = dict(w_in=w_in, g_mix=g_mix, g_ffn=g_ffn, swa_q_norm=swa_q_norm, swa_k_norm=swa_k_norm,
                  mla_cq_norm=mla_cq_norm, mla_ckv_norm=mla_ckv_norm, mla_w_uq=mla_w_uq, mla_w_ukv=mla_w_ukv,
                  mla_q_norm=mla_q_norm, mla_k_norm=mla_k_norm, gm_ln_g=gm_ln_g, gm_ln_b=gm_ln_b,
                  gm_w_s=gm_w_s, gm_b_s=gm_b_s, w_o_a=w_o_a, w_o_b=w_o_b, w_o_c=w_o_c, w_out=w_out,
                  w_rg=w_rg, b_rg=b_rg, w_re=w_re, b_re=b_re, w_gate=w_gate, w_up=w_up, w_down=w_down)
    w_all = _prep_weights(params)
    consts = {
        "ones64": _block_ones(SWA_HEAD_DIM), "ones128": _block_ones(HEAD_PAD),
        "rot_a": _rotate_half_matrix(SWA_HEAD_DIM, 0, SWA_HEAD_DIM),
        "rot_b": _rotate_half_matrix(HEAD_PAD, MLA_NOPE, MLA_ROPE),
        "v_ones": jnp.asarray((np.arange(MLA_PAD_COLS) % HEAD_PAD >= MLA_V).astype(np.float32)[None, :]),
        "tri": jnp.asarray(np.triu(np.ones((moe_tile, moe_tile), np.float32), k=1), BF16),
    }
    cos_a, sin_a = _rope_tables(lat_len, SWA_HEAD_DIM, 0, SWA_HEAD_DIM)
    cos_b, sin_b = _rope_tables(lat_len, HEAD_PAD, MLA_NOPE, MLA_ROPE)
    tables = {"cos_a": cos_a, "sin_a": sin_a, "cos_b": cos_b, "sin_b": sin_b}

    cond8 = jnp.zeros((8, d), F32).at[:n_lat_seq].set(c).at[n_lat_seq].set(c_ctx)
    mods = _adaln(cond8, w_mod, b_mod).reshape(depth, 8, 1, 6 * d)

    k0 = cache_swa_k[:, :, :, 0, :]
    k1 = cache_swa_k[:, :, :, 1, :]
    kd_ctx_all = jnp.concatenate([k0, k0, k1, k1], axis=-1).astype(BF16)
    v_ctx_all = cache_swa_v.reshape(n_lat_seq, depth, past, SWA_KV_COLS).astype(BF16)
    kr_hi = jnp.pad(cache_mla_krope, ((0, 0), (0, 0), (0, 0), (MLA_NOPE, LANES - MLA_QK)))
    kb_ctx_all, vt_ctx_all = _ctx_keys(cache_mla_ckv, kr_hi, w_all["w_kn"], w_all["w_vp"], w_all["gk1"],
                                       w_all["gk2"], consts["v_ones"], consts["ones128"])

    ctx_row = lambda i: n_lat_seq
    lat_row = lambda i: i // (lat_len // t)
    lat_moe_row = lambda i: i // (lat_len // moe_tile)

    x_c = x_prompt.reshape(n_ctx, d)
    x_s = x_sample.reshape(n_lat, d)
    caches = [[], [], [], []]
    for l in range(depth):
        wl = {k: v[l] for k, v in w_all.items()}
        mod_l = mods[l]
        sink = swa_sink[l].reshape(SWA_HEADS)
        (qa_c, kd_c, va_c, qb_c, kb_c, vb_c, oc_c, ka_f, va_f, ckv_f, kr_f) = _token_stage(
            False, x_c, ctx_tiles, mod_l, ctx_row, wl, consts, None)
        (qa_s, kd_s, va_s, qb_s, kb_s, vb_s, oc_s) = _token_stage(
            True, x_s, lat_tiles, mod_l, lat_row, wl, consts, tables)
        for dst, val in zip(caches, (ka_f, va_f, ckv_f, kr_f)):
            dst.append(val)
        oa_c, ob_c = _ctx_attention(sink, qa_c, kd_c, va_c, qb_c, kb_c, vb_c, n_ctx_seq, ctx_len)
        oa_s = _window_attention(sink, qa_s, kd_s, va_s, kd_ctx_all[:, l], v_ctx_all[:, l], n_lat_seq, lat_len)
        ob_s = _latent_mla(qb_s, kb_s, vb_s, kb_ctx_all[l], vt_ctx_all[l], n_lat_seq, lat_len)
        x1_c, h2_c, route_c = _finish(x_c, oa_c, ob_c, oc_c, mod_l, ctx_row, wl)
        x1_s, h2_s, route_s = _finish(x_s, oa_s, ob_s, oc_s, mod_l, lat_row, wl)
        x_c = _moe(h2_c, route_c, x1_c, mod_l, ctx_row, wl, consts["tri"])
        x_s = _moe(h2_s, route_s, x1_s, mod_l, lat_moe_row, wl, consts["tri"])

    y_prompt = x_c.reshape(n_ctx_seq, ctx_len, d)
    y_sample = x_s.reshape(n_lat_seq, lat_len, d)
    stack = lambda vals, tail: jnp.stack([v.reshape((n_ctx_seq, ctx_len) + tail) for v in vals], axis=1)
    return (y_prompt, y_sample,
            stack(caches[0], (SWA_KV_HEADS, SWA_HEAD_DIM)), stack(caches[1], (SWA_KV_HEADS, SWA_HEAD_DIM)),
            stack(caches[2], (MLA_KV_LORA,)), stack(caches[3], (MLA_ROPE,)))
```

```python
import functools

import numpy as np
import jax
import jax.numpy as jnp
from jax import lax
from jax.experimental import pallas as pl
from jax.experimental.pallas import tpu as pltpu

F32 = jnp.float32
BF16 = jnp.bfloat16

D_MODEL = 1024
GRID_W = 64
ROPE_THETA = 10000.0
EPS = 1e-6
NEG_INF = -1e30

SWA_HEADS = 8
SWA_KV_HEADS = 2
SWA_GROUP = SWA_HEADS // SWA_KV_HEADS
SWA_HEAD_DIM = 64
SWA_WINDOW = 128
SWA_Q_COLS = SWA_HEADS * SWA_HEAD_DIM
SWA_KV_COLS = SWA_KV_HEADS * SWA_HEAD_DIM

MLA_HEADS = 8
MLA_Q_LORA = 256
MLA_KV_LORA = 128
MLA_NOPE = 64
MLA_ROPE = 32
MLA_V = 64
MLA_QK = MLA_NOPE + MLA_ROPE

GM_CHUNK = 128
GM_GROUPS = 4
GM_WIDTH = 512

N_EXPERT_GROUPS = 4
EXPERTS_PER_GROUP = 4
N_EXPERTS = N_EXPERT_GROUPS * EXPERTS_PER_GROUP
EXPERT_FF = 512

LANES = 128
MXU_DIM = 256
VMEM_LIMIT_BYTES = 56 * 1024 * 1024

HEAD_PAD = LANES
MLA_PAD_COLS = MLA_HEADS * HEAD_PAD
BF16_SUBLANES = 16
MLA_V_ROWS = MLA_V + BF16_SUBLANES
MLA_VT_ROWS = MLA_HEADS * MLA_V_ROWS

TOKEN_TILE = 512
MLA_Q_TILE = 256
MLA_KEY_CHUNK = 512
MLA_SCORE_LOOKAHEAD = 2
LOG2_E = float(np.log2(np.e))
MOE_TILE_CANDIDATES = (1024, 512, 256)
MOE_CHUNK = 320

_QA0, _KA0, _VA0, _CQ0, _CKV0, _KR0, _U0, _V0, _ZCOLS = 0, 512, 640, 768, 1024, 1152, 1280, 1792, 2304
_GATE0 = SWA_Q_COLS + 2 * SWA_KV_COLS + MLA_Q_LORA + MLA_KV_LORA + MLA_ROPE + 2 * GM_WIDTH

ROUTER_LANE0 = N_EXPERT_GROUPS


def _cparams(*sem):
    return pltpu.CompilerParams(dimension_semantics=sem, vmem_limit_bytes=VMEM_LIMIT_BYTES)


def _nt_dot(a, b):
    return lax.dot_general(a, b, (((1,), (1,)), ((), ())), preferred_element_type=F32)


def _dot(a, b):
    return jnp.dot(a, b, preferred_element_type=F32)


def _gelu_tanh(x):
    return 0.5 * x * (1.0 + jnp.tanh(np.sqrt(2.0 / np.pi) * (x + 0.044715 * (x * x * x))))


def _group_sumsq(v, ones_ref):
    width = ones_ref.shape[0]
    parts = []
    for c in range(v.shape[1] // width):
        blk = v[:, c * width:(c + 1) * width]
        parts.append(_dot((blk * blk).astype(BF16), ones_ref[...]))
    return parts[0] if len(parts) == 1 else jnp.concatenate(parts, axis=1)


def _rope(v, cos, sin, rot_ref):
    width = rot_ref.shape[0]
    parts = []
    for c in range(v.shape[1] // width):
        blk = v[:, c * width:(c + 1) * width]
        rot = _dot(blk.astype(BF16), rot_ref[...])
        for s in range(width // LANES):
            sl = slice(s * LANES, (s + 1) * LANES)
            parts.append(blk[:, sl] * cos + rot[:, sl] * sin)
    return parts[0] if len(parts) == 1 else jnp.concatenate(parts, axis=1)


def _modulated_norm(x, gain, shift, scale):
    ms = jnp.mean(x * x, axis=-1, keepdims=True)
    h = x * lax.rsqrt(ms + EPS) * gain
    return h * (1.0 + scale) + shift


def _adaln_kernel(cond_ref, w_ref, b_ref, o_ref):
    a = cond_ref[...]
    a = a * jax.nn.sigmoid(a)
    o_ref[...] = _dot(a.astype(BF16), w_ref[...].astype(BF16)) + b_ref[...]


def _adaln(cond8, w_mod, b_mod):
    depth, d, n = w_mod.shape
    tn = 1536
    return pl.pallas_call(
        _adaln_kernel,
        grid=(depth, n // tn),
        in_specs=[
            pl.BlockSpec((8, d), lambda l, j: (0, 0)),
            pl.BlockSpec((None, d, tn), lambda l, j: (l, 0, j)),
            pl.BlockSpec((None, 1, tn), lambda l, j: (l, 0, j)),
        ],
        out_specs=pl.BlockSpec((None, 8, tn), lambda l, j: (l, 0, j)),
        out_shape=jax.ShapeDtypeStruct((depth, 8, n), F32),
        compiler_params=_cparams("parallel", "parallel"),
        name="adaln",
    )(cond8, w_mod, b_mod.reshape(depth, 1, n))


def _mla_keys_values(ckv_n, kr_hi, wkn_ref, wv_ref, gk1_ref, gk2_ref, ones128_ref, rope_args):
    cb = ckv_n.astype(BF16)
    nope = _dot(cb, wkn_ref[...])
    vals = _dot(cb, wv_ref[...])
    kr_ss = jnp.sum(kr_hi * kr_hi, axis=-1, keepdims=True)
    inv = lax.rsqrt((_group_sumsq(nope, ones128_ref) + kr_ss) * (1.0 / MLA_QK) + EPS)
    krg = kr_hi * gk2_ref[...]
    if rope_args is not None:
        cos, sin, rot_ref = rope_args
        krg = _rope(krg, cos, sin, rot_ref)
    gk1 = gk1_ref[...]
    parts = []
    for h in range(MLA_HEADS):
        sl = slice(h * HEAD_PAD, (h + 1) * HEAD_PAD)
        parts.append((nope[:, sl] * gk1[:, sl] + krg) * inv[:, sl])
    return jnp.concatenate(parts, axis=1), vals


def _values_with_sum_rows(vals_pad, vone_ref):
    return (vals_pad + vone_ref[...]).T.astype(BF16)


def _ctx_keys_kernel(ckv_ref, kr_ref, wkn_ref, wv_ref, gk1_ref, gk2_ref, vone_ref, ones128_ref, k_ref, vt_ref):
    keys, vals = _mla_keys_values(ckv_ref[...], kr_ref[...], wkn_ref, wv_ref, gk1_ref, gk2_ref,
                                  ones128_ref, None)
    k_ref[...] = keys.astype(BF16)
    vt_ref[...] = _values_with_sum_rows(vals, vone_ref)


def _ctx_keys(cache_ckv, cache_kr_hi, wkn, wvp, gk1, gk2, vone, ones128):
    nb, depth, past, _ = cache_ckv.shape
    cache_map = lambda l, b: (b, l, 0, 0)
    w_map = lambda l, b: (l, 0, 0)
    return pl.pallas_call(
        _ctx_keys_kernel,
        grid=(depth, nb),
        in_specs=[
            pl.BlockSpec((None, None, past, MLA_KV_LORA), cache_map),
            pl.BlockSpec((None, None, past, LANES), cache_map),
            pl.BlockSpec((None, MLA_KV_LORA, MLA_PAD_COLS), w_map),
            pl.BlockSpec((None, MLA_KV_LORA, MLA_VT_ROWS), w_map),
            pl.BlockSpec((None, 1, MLA_PAD_COLS), w_map),
            pl.BlockSpec((None, 1, LANES), w_map),
            pl.BlockSpec((1, MLA_VT_ROWS), lambda l, b: (0, 0)),
            pl.BlockSpec((MXU_DIM, MXU_DIM), lambda l, b: (0, 0)),
        ],
        out_specs=[
            pl.BlockSpec((None, None, past, MLA_PAD_COLS), lambda l, b: (l, b, 0, 0)),
            pl.BlockSpec((None, None, MLA_VT_ROWS, past), lambda l, b: (l, b, 0, 0)),
        ],
        out_shape=[
            jax.ShapeDtypeStruct((depth, nb, past, MLA_PAD_COLS), BF16),
            jax.ShapeDtypeStruct((depth, nb, MLA_VT_ROWS, past), BF16),
        ],
        compiler_params=_cparams("parallel", "parallel"),
        name="ctx_keys",
    )(cache_ckv, cache_kr_hi, wkn, wvp, gk1, gk2, vone, ones128)


def _token_kernel(latent, x_ref, mod_ref, gmix_ref, wa_ref, gqa_ref, gka_ref, gcq_ref, wuq_ref, gq_ref,
                  gckv_ref, wkn_ref, wv_ref, gk1_ref, gk2_ref, lng_ref, lnb_ref, ws_ref, bs_ref,
                  ones64_ref, ones128_ref, *rest):
    if latent:
        (cosa_ref, sina_ref, cosb_ref, sinb_ref, rota_ref, rotb_ref, vone_ref,
         qa_o, kd_o, va_o, qb_o, kb_o, vb_o, oc_o) = rest
    else:
        (qa_o, kd_o, va_o, qb_o, kb_o, vb_o, oc_o, kac_o, vac_o, ckvc_o, krc_o) = rest
    d = D_MODEL
    rows = x_ref.shape[0]
    h = _modulated_norm(x_ref[...], gmix_ref[...], mod_ref[:, 0:d], mod_ref[:, d:2 * d])
    z = _dot(h.astype(BF16), wa_ref[...])

    qa = z[:, _QA0:_KA0]
    qa = qa * lax.rsqrt(_group_sumsq(qa, ones64_ref) * (1.0 / SWA_HEAD_DIM) + EPS) * gqa_ref[...]
    ka = z[:, _KA0:_VA0]
    ka_ss = _dot((ka * ka).astype(BF16), ones64_ref[0:LANES, 0:LANES])
    ka = ka * lax.rsqrt(ka_ss * (1.0 / SWA_HEAD_DIM) + EPS) * gka_ref[...]
    va = z[:, _VA0:_CQ0]
    if latent:
        qa = _rope(qa, cosa_ref[...], sina_ref[...], rota_ref)
        ka_r = _rope(ka, cosa_ref[...], sina_ref[...], rota_ref.at[0:LANES, 0:LANES])
    else:
        kac_o[...] = ka
        vac_o[...] = va
        ka_r = ka
    swapped = pltpu.roll(ka_r, SWA_HEAD_DIM, 1)
    low = lax.broadcasted_iota(jnp.int32, (rows, LANES), 1) < SWA_HEAD_DIM
    qa_o[...] = qa.astype(BF16)
    kd_o[:, 0:LANES] = jnp.where(low, ka_r, swapped).astype(BF16)
    kd_o[:, LANES:2 * LANES] = jnp.where(low, swapped, ka_r).astype(BF16)
    va_o[...] = va.T.astype(BF16) if latent else va.astype(BF16)

    cq = z[:, _CQ0:_CKV0]
    cq = cq * lax.rsqrt(jnp.mean(cq * cq, axis=-1, keepdims=True) + EPS) * gcq_ref[...]
    qb = _dot(cq.astype(BF16), wuq_ref[...])
    qb = qb * lax.rsqrt(_group_sumsq(qb, ones128_ref) * (1.0 / MLA_QK) + EPS) * gq_ref[...]
    if latent:
        qb = _rope(qb, cosb_ref[...], sinb_ref[...], rotb_ref)
    qb_o[...] = qb.astype(BF16)

    ckv = z[:, _CKV0:_KR0]
    ckv = ckv * lax.rsqrt(jnp.mean(ckv * ckv, axis=-1, keepdims=True) + EPS) * gckv_ref[...]
    krb = z[:, _KR0:_U0]
    lane = lax.broadcasted_iota(jnp.int32, (rows, LANES), 1)
    kr_hi = jnp.where(lane >= MLA_NOPE, krb, 0.0)
    rope_args = (cosb_ref[...], sinb_ref[...], rotb_ref.at[0:LANES, 0:LANES]) if latent else None
    kb, vb = _mla_keys_values(ckv, kr_hi, wkn_ref, wv_ref, gk1_ref, gk2_ref, ones128_ref, rope_args)
    kb_o[...] = kb.astype(BF16)
    if latent:
        vb_o[...] = _values_with_sum_rows(vb, vone_ref)
    else:
        vb_o[...] = vb.astype(BF16)
    if not latent:
        ckvc_o[...] = ckv
        krc_o[...] = krb[:, 0:MLA_ROPE]

    gu = _gelu_tanh(z[:, _U0:_V0])
    gv = _gelu_tanh(z[:, _V0:_ZCOLS])
    gc = gv - jnp.mean(gv, axis=-1, keepdims=True)
    var = jnp.mean(gc * gc, axis=-1, keepdims=True)
    vg = (gc * lax.rsqrt(var + EPS) * lng_ref[...] + lnb_ref[...]).astype(BF16)
    n_chunks = rows // GM_CHUNK
    for g in range(GM_GROUPS):
        gl = slice(g * LANES, (g + 1) * LANES)
        rhs = jnp.concatenate([vg[c * GM_CHUNK:(c + 1) * GM_CHUNK, gl] for c in range(n_chunks)], axis=1)
        mixed = _dot(ws_ref[g], rhs)
        for c in range(n_chunks):
            rs = slice(c * GM_CHUNK, (c + 1) * GM_CHUNK)
            oc_o[rs, gl] = (gu[rs, gl] * (mixed[:, c * LANES:(c + 1) * LANES] + bs_ref[:, gl])).astype(BF16)


def _token_stage(latent, x, n_tiles, mod_l, mod_row, wl, consts, tables):
    t = TOKEN_TILE
    d = D_MODEL
    n = n_tiles * t
    full = lambda a: pl.BlockSpec(a.shape, lambda i: (0,) * a.ndim)
    weights = [wl["g_mix"], wl["w_a"], wl["gqa"], wl["gka"], wl["gcq"], wl["w_uq"], wl["gq"], wl["gckv"],
               wl["w_kn"], wl["w_vp"] if latent else wl["w_v"], wl["gk1"], wl["gk2"], wl["ln_g"], wl["ln_b"], wl["w_s"], wl["b_s"],
               consts["ones64"], consts["ones128"]]
    in_specs = [
        pl.BlockSpec((t, d), lambda i: (i, 0)),
        pl.BlockSpec((None, 1, 6 * d), lambda i: (mod_row(i), 0, 0)),
    ] + [full(a) for a in weights]
    args = [x, mod_l] + weights
    row = lambda w: pl.BlockSpec((t, w), lambda i: (i, 0))
    out_widths = [SWA_Q_COLS, 2 * LANES, LANES, MLA_PAD_COLS, MLA_PAD_COLS, MLA_HEADS * MLA_V, GM_WIDTH]
    out_specs = [row(w) for w in out_widths]
    out_shape = [jax.ShapeDtypeStruct((n, w), BF16) for w in out_widths]
    if latent:
        tiles_per_seq = tables["cos_a"].shape[0] // t
        tab = lambda: pl.BlockSpec((t, LANES), lambda i: (i % tiles_per_seq, 0))
        in_specs += [tab(), tab(), tab(), tab(), full(consts["rot_a"]), full(consts["rot_b"]),
                     full(consts["v_ones"])]
        args += [tables["cos_a"], tables["sin_a"], tables["cos_b"], tables["sin_b"],
                 consts["rot_a"], consts["rot_b"], consts["v_ones"]]
        out_specs[5] = pl.BlockSpec((MLA_VT_ROWS, t), lambda i: (0, i))
        out_shape[5] = jax.ShapeDtypeStruct((MLA_VT_ROWS, n), BF16)
        out_specs[2] = pl.BlockSpec((LANES, t), lambda i: (0, i))
        out_shape[2] = jax.ShapeDtypeStruct((LANES, n), BF16)
    else:
        cache_widths = [SWA_KV_COLS, SWA_KV_COLS, MLA_KV_LORA, MLA_ROPE]
        out_specs += [row(w) for w in cache_widths]
        out_shape += [jax.ShapeDtypeStruct((n, w), F32) for w in cache_widths]
    return pl.pallas_call(
        functools.partial(_token_kernel, latent),
        grid=(n_tiles,),
        in_specs=in_specs,
        out_specs=out_specs,
        out_shape=out_shape,
        compiler_params=_cparams("parallel"),
        name="token_stage_latent" if latent else "token_stage_ctx",
    )(*args)


def _softmax_pv(scores, values, extra=None):
    m = jnp.max(scores[0], axis=-1, keepdims=True)
    for s in scores[1:]:
        m = jnp.maximum(m, jnp.max(s, axis=-1, keepdims=True))
    if extra is not None:
        m = jnp.maximum(m, extra)
    den = jnp.exp(extra - m) if extra is not None else 0.0
    acc = None
    for s, v in zip(scores, values):
        p = jnp.exp(s - m)
        den = den + jnp.sum(p, axis=-1, keepdims=True)
        pv = _dot(p.astype(BF16), v)
        acc = pv if acc is None else acc + pv
    return acc / den


def _ctx_attn_kernel(sink_ref, qa_ref, kd_ref, va_ref, qb_ref, kb_ref, vb_ref, oa_ref, ob_ref):
    rows = qa_ref.shape[0]
    low = lax.broadcasted_iota(jnp.int32, (rows, LANES), 1) < SWA_HEAD_DIM
    va = va_ref[...]
    res = []
    for hd in range(SWA_HEADS):
        qblk = qa_ref[:, (hd // 2) * LANES:(hd // 2 + 1) * LANES]
        keep = low if hd % 2 == 0 else jnp.logical_not(low)
        qm = jnp.where(keep, qblk, jnp.zeros_like(qblk))
        kvh = hd // SWA_GROUP
        s = _nt_dot(qm, kd_ref[:, kvh * LANES:(kvh + 1) * LANES]) * (SWA_HEAD_DIM ** -0.5)
        res.append(_softmax_pv([s], [va], sink_ref[hd]))
    for j in range(SWA_GROUP):
        oa_ref[:, j * LANES:(j + 1) * LANES] = jnp.where(low, res[j], res[SWA_GROUP + j]).astype(BF16)
    res = []
    for h in range(MLA_HEADS):
        sl = slice(h * HEAD_PAD, (h + 1) * HEAD_PAD)
        s = _nt_dot(qb_ref[:, sl], kb_ref[:, sl]) * (MLA_QK ** -0.5)
        res.append(_softmax_pv([s], [vb_ref[:, (h // 2) * LANES:(h // 2 + 1) * LANES]]))
    for j in range(MLA_HEADS // 2):
        ob_ref[:, j * LANES:(j + 1) * LANES] = jnp.where(low, res[2 * j], res[2 * j + 1]).astype(BF16)


def _ctx_attention(sink, qa, kd, va, qb, kb, vb, n_seq, seq):
    row = lambda w: pl.BlockSpec((seq, w), lambda b: (b, 0))
    widths = [SWA_Q_COLS, 2 * LANES, LANES, MLA_PAD_COLS, MLA_PAD_COLS, MLA_HEADS * MLA_V]
    return pl.pallas_call(
        _ctx_attn_kernel,
        grid=(n_seq,),
        in_specs=[pl.BlockSpec(memory_space=pltpu.SMEM)] + [row(w) for w in widths],
        out_specs=[row(SWA_Q_COLS), row(MLA_HEADS * MLA_V)],
        out_shape=[jax.ShapeDtypeStruct((n_seq * seq, SWA_Q_COLS), BF16),
                   jax.ShapeDtypeStruct((n_seq * seq, MLA_HEADS * MLA_V), BF16)],
        compiler_params=_cparams("parallel"),
        name="ctx_attention",
    )(sink, qa, kd, va, qb, kb, vb)


def _window_attn_kernel(sink_ref, q_ref, kp_ref, kc_ref, kn_ref, vtp_ref, vtc_ref, vtn_ref, kctx_ref,
                        vtctx_ref, o_ref):
    w = SWA_WINDOW
    i = pl.program_id(1)
    nb = pl.num_programs(1)
    cols = SWA_GROUP * w
    c = (SWA_HEAD_DIM ** -0.5) * LOG2_E
    low = lax.broadcasted_iota(jnp.int32, (w, LANES), 1) < SWA_HEAD_DIM
    m_io = lax.broadcasted_iota(jnp.int32, (3 * w, cols), 0)
    rel = m_io - (lax.broadcasted_iota(jnp.int32, (3 * w, cols), 1) & (w - 1))
    first_row = jnp.where(i == 0, w, 0)
    end_row = jnp.where(i == nb - 1, 2 * w, 3 * w)
    valid = (rel >= 0) & (rel <= 2 * w) & (m_io >= first_row) & (m_io < end_row)
    head_of_col = lax.broadcasted_iota(jnp.int32, (1, cols), 1) >> (w.bit_length() - 1)
    vt_loc = jnp.concatenate([vtp_ref[...], vtc_ref[...], vtn_ref[...]], axis=1)
    vt_ctx = vtctx_ref[...]
    top_loc = lax.broadcasted_iota(jnp.int32, vt_loc.shape, 0) < SWA_HEAD_DIM
    top_ctx = lax.broadcasted_iota(jnp.int32, vt_ctx.shape, 0) < SWA_HEAD_DIM
    staged = []
    for kvh in range(SWA_KV_HEADS):
        qs = []
        sink = jnp.zeros((1, cols), F32)
        for g in range(SWA_GROUP):
            hd = kvh * SWA_GROUP + g
            qblk = q_ref[:, (hd // 2) * LANES:(hd // 2 + 1) * LANES]
            keep = low if hd % 2 == 0 else jnp.logical_not(low)
            qs.append(jnp.where(keep, qblk, jnp.zeros_like(qblk)))
            sink = jnp.where(head_of_col == g, sink_ref[hd], sink)
        qs = jnp.concatenate(qs, axis=0)
        ks = slice(kvh * LANES, (kvh + 1) * LANES)
        k_loc = jnp.concatenate([kp_ref[:, ks], kc_ref[:, ks], kn_ref[:, ks]], axis=0)
        st_ctx = _nt_dot(kctx_ref[:, ks], qs)
        st_loc = jnp.where(valid, _nt_dot(k_loc, qs), NEG_INF)
        staged.append((st_ctx, st_loc, sink))
    res = []
    for kvh, (st_ctx, st_loc, sink) in enumerate(staged):
        top = jnp.maximum(jnp.max(st_ctx, axis=0, keepdims=True), jnp.max(st_loc, axis=0, keepdims=True))
        m2 = jnp.maximum(top * (SWA_HEAD_DIM ** -0.5), sink) * LOG2_E
        p_ctx = jnp.exp2(st_ctx * c - m2).astype(BF16)
        p_loc = jnp.exp2(st_loc * c - m2).astype(BF16)
        own_loc = top_loc if kvh == 0 else jnp.logical_not(top_loc)
        own_ctx = top_ctx if kvh == 0 else jnp.logical_not(top_ctx)
        acc = (_dot(jnp.where(own_ctx, vt_ctx, jnp.ones_like(vt_ctx)), p_ctx)
               + _dot(jnp.where(own_loc, vt_loc, jnp.ones_like(vt_loc)), p_loc))
        own = slice(kvh * SWA_HEAD_DIM, (kvh + 1) * SWA_HEAD_DIM)
        other = (1 - kvh) * SWA_HEAD_DIM
        den = acc[other:other + 1] + jnp.exp2(sink * LOG2_E - m2)
        res.append(acc[own] / den)
    for j in range(SWA_GROUP):
        cs = slice(j * w, (j + 1) * w)
        o_ref[:, j * LANES:(j + 1) * LANES] = jnp.concatenate([res[0][:, cs], res[1][:, cs]], axis=0).T.astype(BF16)


def _window_attention(sink, qa, kd, vat, kd_ctx, vt_ctx, n_seq, seq):
    w = SWA_WINDOW
    nb = seq // w
    past = kd_ctx.shape[1]
    cur = lambda b, i: (b * nb + i, 0)
    prev = lambda b, i: (b * nb + jnp.maximum(i - 1, 0), 0)
    nxt = lambda b, i: (b * nb + jnp.minimum(i + 1, nb - 1), 0)
    flip = lambda f: (lambda b, i: f(b, i)[::-1])
    return pl.pallas_call(
        _window_attn_kernel,
        grid=(n_seq, nb),
        in_specs=[
            pl.BlockSpec(memory_space=pltpu.SMEM),
            pl.BlockSpec((w, SWA_Q_COLS), cur),
            pl.BlockSpec((w, 2 * LANES), prev),
            pl.BlockSpec((w, 2 * LANES), cur),
            pl.BlockSpec((w, 2 * LANES), nxt),
            pl.BlockSpec((LANES, w), flip(prev)),
            pl.BlockSpec((LANES, w), flip(cur)),
            pl.BlockSpec((LANES, w), flip(nxt)),
            pl.BlockSpec((None, past, 2 * LANES), lambda b, i: (b, 0, 0)),
            pl.BlockSpec((None, LANES, past), lambda b, i: (b, 0, 0)),
        ],
        out_specs=pl.BlockSpec((w, SWA_Q_COLS), cur),
        out_shape=jax.ShapeDtypeStruct((n_seq * seq, SWA_Q_COLS), BF16),
        compiler_params=_cparams("parallel", "parallel"),
        name="window_attention",
    )(sink, qa, kd, kd, kd, vat, vat, vat, kd_ctx, vt_ctx)


def _latent_mla_kernel(q_ref, k_ref, vt_ref, kctx_ref, vtctx_ref, o_ref):
    kc = MLA_KEY_CHUNK
    c = (MLA_QK ** -0.5) * LOG2_E
    chunks = ([(kctx_ref, vtctx_ref, j) for j in range(kctx_ref.shape[0] // kc)]
              + [(k_ref, vt_ref, j) for j in range(k_ref.shape[0] // kc)])
    heads = [slice(e * HEAD_PAD, (e + 1) * HEAD_PAD) for e in range(2)]

    def scores(n):
        keys_ref, _, j = chunks[n]
        return [_nt_dot(keys_ref[j * kc:(j + 1) * kc, sl], q_ref[:, sl]) for sl in heads]

    m = [None, None]
    acc = [None, None]
    pending = [scores(n) for n in range(min(MLA_SCORE_LOOKAHEAD, len(chunks)))]
    for n, (_, vals_ref, j) in enumerate(chunks):
        if n + MLA_SCORE_LOOKAHEAD < len(chunks):
            pending.append(scores(n + MLA_SCORE_LOOKAHEAD))
        st = pending.pop(0)
        for e, sl in enumerate(heads):
            cmax = jnp.max(st[e], axis=0, keepdims=True)
            vals = vals_ref[e * MLA_V_ROWS:(e + 1) * MLA_V_ROWS, j * kc:(j + 1) * kc]
            if n == 0:
                m[e] = cmax
                acc[e] = _dot(vals, jnp.exp2((st[e] - cmax) * c).astype(BF16))
            else:
                m_new = jnp.maximum(m[e], cmax)
                p = jnp.exp2((st[e] - m_new) * c).astype(BF16)
                acc[e] = acc[e] * jnp.exp2((m[e] - m_new) * c) + _dot(vals, p)
                m[e] = m_new
    outs = [a[0:MLA_V] / a[MLA_V:MLA_V + 1] for a in acc]
    o_ref[...] = jnp.concatenate(outs, axis=0).T.astype(BF16)


def _latent_mla(qb, kb, vb, k_ctx, v_ctx, n_seq, seq):
    tq = MLA_Q_TILE
    nq = seq // tq
    past = k_ctx.shape[1]
    pair = 2 * HEAD_PAD
    assert past % MLA_KEY_CHUNK == 0 and seq % MLA_KEY_CHUNK == 0
    return pl.pallas_call(
        _latent_mla_kernel,
        grid=(n_seq, MLA_HEADS // 2, nq),
        in_specs=[
            pl.BlockSpec((tq, pair), lambda b, p, i: (b * nq + i, p)),
            pl.BlockSpec((seq, pair), lambda b, p, i: (b, p)),
            pl.BlockSpec((2 * MLA_V_ROWS, seq), lambda b, p, i: (p, b)),
            pl.BlockSpec((None, past, pair), lambda b, p, i: (b, 0, p)),
            pl.BlockSpec((None, 2 * MLA_V_ROWS, past), lambda b, p, i: (b, p, 0)),
        ],
        out_specs=pl.BlockSpec((tq, LANES), lambda b, p, i: (b * nq + i, p)),
        out_shape=jax.ShapeDtypeStruct((n_seq * seq, MLA_HEADS * MLA_V), BF16),
        compiler_params=_cparams("parallel", "parallel", "parallel"),
        name="latent_mla",
    )(qb, kb, vb, k_ctx, v_ctx)


def _route(logits):
    lane = lax.broadcasted_iota(jnp.int32, logits.shape, 1).astype(F32)
    big = float(4 * LANES)
    gmask = lane < N_EXPERT_GROUPS
    gmax = jnp.max(jnp.where(gmask, logits, -jnp.inf), axis=-1, keepdims=True)
    gsum = jnp.sum(jnp.where(gmask, jnp.exp(logits - gmax), 0.0), axis=-1, keepdims=True)
    g_w = 1.0 / gsum
    g_idx = jnp.min(jnp.where(gmask & (logits == gmax), lane, big), axis=-1, keepdims=True)
    first = ROUTER_LANE0 + EXPERTS_PER_GROUP * g_idx
    emask = (lane >= first) & (lane < first + EXPERTS_PER_GROUP)
    el = jnp.where(emask, logits, -jnp.inf)
    m1 = jnp.max(el, axis=-1, keepdims=True)
    i1 = jnp.min(jnp.where(emask & (el == m1), lane, big), axis=-1, keepdims=True)
    el2 = jnp.where(lane == i1, -jnp.inf, el)
    m2 = jnp.max(el2, axis=-1, keepdims=True)
    i2 = jnp.min(jnp.where(emask & (el2 == m2), lane, big), axis=-1, keepdims=True)
    esum = jnp.sum(jnp.where(emask, jnp.exp(logits - m1), 0.0), axis=-1, keepdims=True)
    p1 = 1.0 / esum
    p2 = jnp.exp(m2 - m1) / esum
    tot = p1 + p2
    combine = jnp.where(lane == i1, g_w * (p1 / tot), 0.0) + jnp.where(lane == i2, g_w * (p2 / tot), 0.0)
    return jnp.where(lane == g_idx, 1.0, combine)


def _finish_kernel(x_ref, oa_ref, ob_ref, oc_ref, mod_ref, gmix_ref, gffn_ref, wg_ref, woa_ref, wob_ref,
                   woc_ref, wout_ref, wrh_ref, wrl_ref, br_ref, x1_ref, h2_ref, comb_ref):
    d = D_MODEL
    x = x_ref[...]
    h = _modulated_norm(x, gmix_ref[...], mod_ref[:, 0:d], mod_ref[:, d:2 * d]).astype(BF16)
    gates = _dot(h, wg_ref[...])
    merged = (jax.nn.sigmoid(gates[:, 0:d]) * _dot(oa_ref[...], woa_ref[...])
              + jax.nn.sigmoid(gates[:, d:2 * d]) * _dot(ob_ref[...], wob_ref[...])
              + jax.nn.sigmoid(gates[:, 2 * d:3 * d]) * _dot(oc_ref[...], woc_ref[...]))
    x1 = x + mod_ref[:, 2 * d:3 * d] * _dot(merged.astype(BF16), wout_ref[...])
    x1_ref[...] = x1
    h2 = _modulated_norm(x1, gffn_ref[...], mod_ref[:, 3 * d:4 * d], mod_ref[:, 4 * d:5 * d])
    h2_hi = h2.astype(BF16)
    h2_lo = (h2 - h2_hi.astype(F32)).astype(BF16)
    h2_ref[...] = h2_hi
    logits = (_dot(h2_hi, wrh_ref[...]) + _dot(h2_lo, wrh_ref[...]) + _dot(h2_hi, wrl_ref[...])) + br_ref[...]
    comb_ref[...] = _route(logits)


def _finish(x_all, oa, ob, oc, mod_l, mod_row, wl):
    t = TOKEN_TILE
    d = D_MODEL
    n = x_all.shape[0]
    full = lambda a: pl.BlockSpec(a.shape, lambda i: (0,) * a.ndim)
    row = lambda w: pl.BlockSpec((t, w), lambda i: (i, 0))
    weights = [wl["g_mix"], wl["g_ffn"], wl["w_g"], wl["w_o_a"], wl["w_o_b"], wl["w_o_c"], wl["w_out"],
               wl["w_r_hi"], wl["w_r_lo"], wl["b_r"]]
    return pl.pallas_call(
        _finish_kernel,
        grid=(n // t,),
        in_specs=[row(d), row(SWA_Q_COLS), row(MLA_HEADS * MLA_V), row(GM_WIDTH),
                  pl.BlockSpec((None, 1, 6 * d), lambda i: (mod_row(i), 0, 0))] + [full(a) for a in weights],
        out_specs=[row(d), row(d), row(LANES)],
        out_shape=[jax.ShapeDtypeStruct((n, d), F32), jax.ShapeDtypeStruct((n, d), BF16),
                   jax.ShapeDtypeStruct((n, LANES), F32)],
        compiler_params=_cparams("parallel"),
        name="finish",
    )(x_all, oa, ob, oc, mod_l, *weights)


def _moe_kernel(h_ref, route_ref, x1_ref, mod_ref, tri_ref, wg_ref, wu_ref, wd_ref, o_ref,
                route_t, rank_rows, rank_cols, xg, yacc, wsel):
    d = D_MODEL
    t = h_ref.shape[0]
    ch = MOE_CHUNK
    e = pl.program_id(1)
    g = lax.shift_right_logical(e, 2)
    j = e & (EXPERTS_PER_GROUP - 1)

    @pl.when(e == 0)
    def _():
        rt = route_ref[...].T
        route_t[...] = rt
        rr = _dot(rt.astype(BF16), tri_ref[...])
        rank_rows[...] = rr
        rank_cols[...] = rr.T
        o_ref[...] = jnp.zeros_like(o_ref)

    ind_row = route_t[pl.ds(g, 1), :]
    n_rows = jnp.sum(ind_row).astype(jnp.int32)
    n_chunks = (n_rows + (ch - 1)) // ch

    def chunk_rows(c):
        return pl.ds(pl.multiple_of(c * ch, ch), ch)

    @pl.when(j == 0)
    def _():
        rank_row = rank_rows[pl.ds(g, 1), :]
        member = ind_row > 0.5
        r_io = lax.broadcasted_iota(jnp.int32, (ch, t), 0).astype(F32)
        lane = lax.broadcasted_iota(jnp.int32, (ch, LANES), 1)

        def gather(c, carry):
            sel = (rank_row == r_io + (c * ch).astype(F32)) & member
            rows = chunk_rows(c)
            xg[rows, :] = _dot(jnp.where(sel, 1.0, 0.0).astype(BF16), h_ref[...]).astype(BF16)
            wt = jnp.zeros((ch, LANES), F32)
            for jj in range(EXPERTS_PER_GROUP):
                w_row = route_t[pl.ds(ROUTER_LANE0 + EXPERTS_PER_GROUP * g + jj, 1), :]
                w_col = jnp.sum(jnp.where(sel, w_row, 0.0), axis=-1, keepdims=True)
                wt = jnp.where(lane == jj, w_col, wt)
            wsel[rows, :] = wt
            yacc[rows, :] = jnp.zeros((ch, d), F32)
            return carry

        lax.fori_loop(0, n_chunks, gather, 0)

    def ffn(c, carry):
        rows = chunk_rows(c)
        x = xg[rows, :]
        a = _dot(x, wg_ref[...])
        hid = (a * jax.nn.sigmoid(a)) * _dot(x, wu_ref[...])
        y = _dot(hid.astype(BF16), wd_ref[...])
        lane = lax.broadcasted_iota(jnp.int32, (ch, LANES), 1)
        w_col = jnp.sum(jnp.where(lane == j, wsel[rows, :], 0.0), axis=-1, keepdims=True)
        yacc[rows, :] += w_col * y
        return carry

    lax.fori_loop(0, n_chunks, ffn, 0)

    @pl.when(j == EXPERTS_PER_GROUP - 1)
    def _():
        lane_t = lax.broadcasted_iota(jnp.int32, (t, LANES), 1)
        member = jnp.sum(jnp.where(lane_t == g, route_ref[...], 0.0), axis=-1, keepdims=True) > 0.5
        rank_col = jnp.sum(jnp.where(lane_t == g, rank_cols[...], 0.0), axis=-1, keepdims=True)
        c_io = lax.broadcasted_iota(jnp.int32, (t, ch), 1).astype(F32)

        def scatter(c, carry):
            sel_t = (rank_col == c_io + (c * ch).astype(F32)) & member
            o_ref[...] += _dot(jnp.where(sel_t, 1.0, 0.0).astype(BF16), yacc[chunk_rows(c), :].astype(BF16))
            return carry

        lax.fori_loop(0, n_chunks, scatter, 0)

    @pl.when(e == pl.num_programs(1) - 1)
    def _():
        o_ref[...] = x1_ref[...] + mod_ref[:, 5 * d:6 * d] * o_ref[...]


def _moe(h2, route, x1, mod_l, mod_row, wl, tri):
    d = D_MODEL
    n = h2.shape[0]
    tile = tri.shape[0]
    chunk_rows = pl.cdiv(tile, MOE_CHUNK) * MOE_CHUNK
    row = lambda w: pl.BlockSpec((tile, w), lambda i, e: (i, 0))
    return pl.pallas_call(
        _moe_kernel,
        grid=(n // tile, N_EXPERTS),
        in_specs=[row(d), row(LANES), row(d),
                  pl.BlockSpec((None, 1, 6 * d), lambda i, e: (mod_row(i), 0, 0)),
                  pl.BlockSpec((tile, tile), lambda i, e: (0, 0)),
                  pl.BlockSpec((None, d, EXPERT_FF), lambda i, e: (e, 0, 0)),
                  pl.BlockSpec((None, d, EXPERT_FF), lambda i, e: (e, 0, 0)),
                  pl.BlockSpec((None, EXPERT_FF, d), lambda i, e: (e, 0, 0))],
        out_specs=row(d),
        out_shape=jax.ShapeDtypeStruct((n, d), F32),
        scratch_shapes=[pltpu.VMEM((LANES, tile), F32), pltpu.VMEM((LANES, tile), F32),
                        pltpu.VMEM((tile, LANES), F32), pltpu.VMEM((chunk_rows, d), BF16),
                        pltpu.VMEM((chunk_rows, d), F32), pltpu.VMEM((chunk_rows, LANES), F32)],
        compiler_params=_cparams("parallel", "arbitrary"),
        name="moe_grouped",
    )(h2, route, x1, mod_l, tri, wl["w_gate"], wl["w_up"], wl["w_down"])


def _block_ones(group):
    idx = np.arange(MXU_DIM)
    return jnp.asarray((idx[:, None] // group) == (idx[None, :] // group), BF16)


def _rotate_half_matrix(block, rot_start, rot_dim):
    half = rot_dim // 4
    r = np.zeros((MXU_DIM, MXU_DIM), np.float32)
    for j in range(MXU_DIM):
        o = j % block - rot_start
        if 0 <= o < rot_dim:
            partner = j + half if (o % (2 * half)) < half else j - half
            r[partner, j] = 1.0
    return jnp.asarray(r, BF16)


def _rope_tables(n_tokens, block, rot_start, rot_dim):
    half = rot_dim // 4
    tok = np.arange(n_tokens)
    pos = np.stack([tok // GRID_W, tok % GRID_W], axis=1).astype(np.float32)
    inv_freq = (ROPE_THETA ** (-np.arange(half, dtype=np.float32) / half)).astype(np.float32)
    lane = np.arange(LANES)
    o = lane % block - rot_start
    rot = (o >= 0) & (o < rot_dim)
    oc = np.where(rot, o, 0)
    axis = oc // (2 * half)
    freq = inv_freq[oc % half]
    sign = np.where((oc % (2 * half)) < half, -1.0, 1.0).astype(np.float32)
    ang = jnp.asarray(pos[:, axis]) * jnp.asarray(freq)[None, :]
    rot_j = jnp.asarray(rot)[None, :]
    cos = jnp.where(rot_j, jnp.cos(ang), 1.0)
    sin = jnp.where(rot_j, jnp.sin(ang) * jnp.asarray(sign)[None, :], 0.0)
    return cos.astype(F32), sin.astype(F32)


def _pad_heads(w, heads, width, padded=HEAD_PAD):
    lead = w.shape[:-1]
    w = w.reshape(lead + (heads, width))
    w = jnp.pad(w, [(0, 0)] * len(lead) + [(0, 0), (0, padded - width)])
    return w.reshape(lead + (heads * padded,))


def _prep_weights(p):
    depth = p["w_in"].shape[0]
    w_in = p["w_in"]
    c = np.cumsum((0, SWA_Q_COLS, SWA_KV_COLS, SWA_KV_COLS, MLA_Q_LORA, MLA_KV_LORA, MLA_ROPE, GM_WIDTH,
                   GM_WIDTH))
    kr = w_in[..., c[5]:c[6]]
    zero = jnp.zeros_like(kr)
    w_a = jnp.concatenate([w_in[..., c[0]:c[5]], kr, zero, kr, zero, w_in[..., c[6]:c[8]]], axis=-1)
    ukv = p["mla_w_ukv"].reshape(depth, MLA_KV_LORA, MLA_HEADS, MLA_NOPE + MLA_V)
    order = np.array([0, 4, 1, 5, 2, 6, 3, 7])
    w_o_a = p["w_o_a"].reshape(depth, SWA_HEADS, SWA_HEAD_DIM, D_MODEL)[:, order].reshape(depth, SWA_Q_COLS,
                                                                                          D_MODEL)
    w_r = jnp.concatenate([p["w_rg"], p["w_re"]], axis=-1)
    w_r = jnp.pad(w_r, ((0, 0), (0, 0), (0, LANES - w_r.shape[-1])))
    w_r_hi = w_r.astype(BF16)
    b_r = jnp.concatenate([p["b_rg"], p["b_re"]], axis=-1)
    b_r = jnp.pad(b_r, ((0, 0), (0, LANES - b_r.shape[-1])))
    k_gain = p["mla_k_norm"]
    row = lambda a: a.reshape(depth, 1, a.shape[-1]).astype(F32)
    bs_full = jnp.repeat(jnp.swapaxes(p["gm_b_s"], 1, 2), LANES, axis=-1)
    return {
        "g_mix": row(p["g_mix"]), "g_ffn": row(p["g_ffn"]),
        "w_a": w_a.astype(BF16), "w_g": w_in[..., _GATE0:].astype(BF16),
        "gqa": row(jnp.tile(p["swa_q_norm"], (1, SWA_HEADS))),
        "gka": row(jnp.tile(p["swa_k_norm"], (1, SWA_KV_HEADS))),
        "gcq": row(p["mla_cq_norm"]), "gckv": row(p["mla_ckv_norm"]),
        "w_uq": _pad_heads(p["mla_w_uq"], MLA_HEADS, MLA_QK).astype(BF16),
        "gq": row(_pad_heads(jnp.tile(p["mla_q_norm"], (1, MLA_HEADS)), MLA_HEADS, MLA_QK)),
        "w_kn": _pad_heads(ukv[..., :MLA_NOPE].reshape(depth, MLA_KV_LORA, -1), MLA_HEADS, MLA_NOPE).astype(BF16),
        "w_v": ukv[..., MLA_NOPE:].reshape(depth, MLA_KV_LORA, -1).astype(BF16),
        "w_vp": _pad_heads(ukv[..., MLA_NOPE:].reshape(depth, MLA_KV_LORA, -1), MLA_HEADS, MLA_V,
                           MLA_V_ROWS).astype(BF16),
        "gk1": row(_pad_heads(jnp.tile(k_gain[:, :MLA_NOPE], (1, MLA_HEADS)), MLA_HEADS, MLA_NOPE)),
        "gk2": row(jnp.pad(k_gain[:, MLA_NOPE:], ((0, 0), (MLA_NOPE, LANES - MLA_QK)))),
        "ln_g": row(p["gm_ln_g"]), "ln_b": row(p["gm_ln_b"]),
        "w_s": p["gm_w_s"].astype(BF16), "b_s": bs_full.astype(F32),
        "w_o_a": w_o_a.astype(BF16), "w_o_b": p["w_o_b"].astype(BF16), "w_o_c": p["w_o_c"].astype(BF16),
        "w_out": p["w_out"].astype(BF16),
        "w_r_hi": w_r_hi, "w_r_lo": (w_r - w_r_hi.astype(F32)).astype(BF16), "b_r": row(b_r),
        "w_gate": p["w_gate"].astype(BF16), "w_up": p["w_up"].astype(BF16), "w_down": p["w_down"].astype(BF16),
    }


def kernel(x_prompt, x_sample, cache_swa_k, cache_swa_v, cache_mla_ckv, cache_mla_krope, c, c_ctx, w_mod, b_mod, g_mix, g_ffn, w_in, swa_q_norm, swa_k_norm, swa_sink, mla_cq_norm, mla_ckv_norm, mla_w_uq, mla_w_ukv, mla_q_norm, mla_k_norm, gm_ln_g, gm_ln_b, gm_w_s, gm_b_s, w_o_a, w_o_b, w_o_c, w_out, w_rg, b_rg, w_re, b_re, w_gate, w_up, w_down):
    d = D_MODEL
    n_ctx_seq, ctx_len, _ = x_prompt.shape
    n_lat_seq, lat_len, _ = x_sample.shape
    depth = w_in.shape[0]
    past = cache_swa_k.shape[2]
    n_ctx = n_ctx_seq * ctx_len
    n_lat = n_lat_seq * lat_len
    t = TOKEN_TILE
    assert n_ctx % t == 0 and ctx_len % GM_CHUNK == 0 and lat_len % t == 0 and lat_len % MLA_Q_TILE == 0 and lat_len % GRID_W == 0 and n_lat_seq < 8
    ctx_tiles = n_ctx // t
    lat_tiles = n_lat // t
    moe_tile = next(m for m in MOE_TILE_CANDIDATES if n_ctx % m == 0 and lat_len % m == 0)

    params = dict(w_in=w_in, g_mix=g_mix, g_ffn=g_ffn, swa_q_norm=swa_q_norm, swa_k_norm=swa_k_norm,
                  mla_cq_norm=mla_cq_norm, mla_ckv_norm=mla_ckv_norm, mla_w_uq=mla_w_uq, mla_w_ukv=mla_w_ukv,
                  mla_q_norm=mla_q_norm, mla_k_norm=mla_k_norm, gm_ln_g=gm_ln_g, gm_ln_b=gm_ln_b,
                  gm_w_s=gm_w_s, gm_b_s=gm_b_s, w_o_a=w_o_a, w_o_b=w_o_b, w_o_c=w_o_c, w_out=w_out,
                  w_rg=w_rg, b_rg=b_rg, w_re=w_re, b_re=b_re, w_gate=w_gate, w_up=w_up, w_down=w_down)
    w_all = _prep_weights(params)
    consts = {
        "ones64": _block_ones(SWA_HEAD_DIM), "ones128": _block_ones(HEAD_PAD),
        "rot_a": _rotate_half_matrix(SWA_HEAD_DIM, 0, SWA_HEAD_DIM),
        "rot_b": _rotate_half_matrix(HEAD_PAD, MLA_NOPE, MLA_ROPE),
        "v_ones": jnp.asarray((np.arange(MLA_VT_ROWS) % MLA_V_ROWS >= MLA_V).astype(np.float32)[None, :]),
        "tri": jnp.asarray(np.triu(np.ones((moe_tile, moe_tile), np.float32), k=1), BF16),
    }
    cos_a, sin_a = _rope_tables(lat_len, SWA_HEAD_DIM, 0, SWA_HEAD_DIM)
    cos_b, sin_b = _rope_tables(lat_len, HEAD_PAD, MLA_NOPE, MLA_ROPE)
    tables = {"cos_a": cos_a, "sin_a": sin_a, "cos_b": cos_b, "sin_b": sin_b}

    cond8 = jnp.zeros((8, d), F32).at[:n_lat_seq].set(c).at[n_lat_seq].set(c_ctx)
    mods = _adaln(cond8, w_mod, b_mod).reshape(depth, 8, 1, 6 * d)

    k0 = cache_swa_k[:, :, :, 0, :]
    k1 = cache_swa_k[:, :, :, 1, :]
    kd_ctx_all = jnp.concatenate([k0, k0, k1, k1], axis=-1).astype(BF16)
    vt_swa_ctx_all = jnp.swapaxes(cache_swa_v.reshape(n_lat_seq, depth, past, SWA_KV_COLS), 2, 3).astype(BF16)
    kr_hi = jnp.pad(cache_mla_krope, ((0, 0), (0, 0), (0, 0), (MLA_NOPE, LANES - MLA_QK)))
    kb_ctx_all, vt_ctx_all = _ctx_keys(cache_mla_ckv, kr_hi, w_all["w_kn"], w_all["w_vp"], w_all["gk1"],
                                       w_all["gk2"], consts["v_ones"], consts["ones128"])

    ctx_row = lambda i: n_lat_seq
    lat_row = lambda i: i // (lat_len // t)
    lat_moe_row = lambda i: i // (lat_len // moe_tile)

    x_c = x_prompt.reshape(n_ctx, d)
    x_s = x_sample.reshape(n_lat, d)
    caches = [[], [], [], []]
    for l in range(depth):
        wl = {k: v[l] for k, v in w_all.items()}
        mod_l = mods[l]
        sink = swa_sink[l].reshape(SWA_HEADS)
        (qa_c, kd_c, va_c, qb_c, kb_c, vb_c, oc_c, ka_f, va_f, ckv_f, kr_f) = _token_stage(
            False, x_c, ctx_tiles, mod_l, ctx_row, wl, consts, None)
        (qa_s, kd_s, va_s, qb_s, kb_s, vb_s, oc_s) = _token_stage(
            True, x_s, lat_tiles, mod_l, lat_row, wl, consts, tables)
        for dst, val in zip(caches, (ka_f, va_f, ckv_f, kr_f)):
            dst.append(val)
        oa_c, ob_c = _ctx_attention(sink, qa_c, kd_c, va_c, qb_c, kb_c, vb_c, n_ctx_seq, ctx_len)
        oa_s = _window_attention(sink, qa_s, kd_s, va_s, kd_ctx_all[:, l], vt_swa_ctx_all[:, l], n_lat_seq,
                                 lat_len)
        ob_s = _latent_mla(qb_s, kb_s, vb_s, kb_ctx_all[l], vt_ctx_all[l], n_lat_seq, lat_len)
        x1_c, h2_c, route_c = _finish(x_c, oa_c, ob_c, oc_c, mod_l, ctx_row, wl)
        x1_s, h2_s, route_s = _finish(x_s, oa_s, ob_s, oc_s, mod_l, lat_row, wl)
        x_c = _moe(h2_c, route_c, x1_c, mod_l, ctx_row, wl, consts["tri"])
        x_s = _moe(h2_s, route_s, x1_s, mod_l, lat_moe_row, wl, consts["tri"])

    y_prompt = x_c.reshape(n_ctx_seq, ctx_len, d)
    y_sample = x_s.reshape(n_lat_seq, lat_len, d)
    stack = lambda vals, tail: jnp.stack([v.reshape((n_ctx_seq, ctx_len) + tail) for v in vals], axis=1)
    return (y_prompt, y_sample,
            stack(caches[0], (SWA_KV_HEADS, SWA_HEAD_DIM)), stack(caches[1], (SWA_KV_HEADS, SWA_HEAD_DIM)),
            stack(caches[2], (MLA_KV_LORA,)), stack(caches[3], (MLA_ROPE,)))
```

```python
import functools

import numpy as np
import jax
import jax.numpy as jnp
from jax import lax
from jax.experimental import pallas as pl
from jax.experimental.pallas import tpu as pltpu

F32 = jnp.float32
BF16 = jnp.bfloat16

D_MODEL = 1024
GRID_W = 64
ROPE_THETA = 10000.0
EPS = 1e-6
NEG_INF = -1e30

SWA_HEADS = 8
SWA_KV_HEADS = 2
SWA_GROUP = SWA_HEADS // SWA_KV_HEADS
SWA_HEAD_DIM = 64
SWA_WINDOW = 128
SWA_Q_COLS = SWA_HEADS * SWA_HEAD_DIM
SWA_KV_COLS = SWA_KV_HEADS * SWA_HEAD_DIM

MLA_HEADS = 8
MLA_Q_LORA = 256
MLA_KV_LORA = 128
MLA_NOPE = 64
MLA_ROPE = 32
MLA_V = 64
MLA_QK = MLA_NOPE + MLA_ROPE

GM_CHUNK = 128
GM_GROUPS = 4
GM_WIDTH = 512

N_EXPERT_GROUPS = 4
EXPERTS_PER_GROUP = 4
N_EXPERTS = N_EXPERT_GROUPS * EXPERTS_PER_GROUP
EXPERT_FF = 512

LANES = 128
MXU_DIM = 256
VMEM_LIMIT_BYTES = 56 * 1024 * 1024

HEAD_PAD = LANES
MLA_PAD_COLS = MLA_HEADS * HEAD_PAD
MLA_V_ROWS = HEAD_PAD
MLA_VT_ROWS = MLA_HEADS * MLA_V_ROWS

TOKEN_TILE = 512
MLA_Q_TILE = 512
MLA_KEY_CHUNK = 512
MLA_SCORE_LOOKAHEAD = 2
LOG2_E = float(np.log2(np.e))
MOE_TILE_CANDIDATES = (1024, 512, 256)
MOE_CHUNK = 320
MOE_EXPERTS_PER_STEP = 2

_QA0, _KA0, _VA0, _CQ0, _CKV0, _KR0, _U0, _V0, _ZCOLS = 0, 512, 640, 768, 1024, 1152, 1280, 1792, 2304
_GATE0 = SWA_Q_COLS + 2 * SWA_KV_COLS + MLA_Q_LORA + MLA_KV_LORA + MLA_ROPE + 2 * GM_WIDTH

ROUTER_LANE0 = N_EXPERT_GROUPS


def _cparams(*sem):
    return pltpu.CompilerParams(dimension_semantics=sem, vmem_limit_bytes=VMEM_LIMIT_BYTES)


def _nt_dot(a, b):
    return lax.dot_general(a, b, (((1,), (1,)), ((), ())), preferred_element_type=F32)


def _dot(a, b):
    return jnp.dot(a, b, preferred_element_type=F32)


def _gelu_tanh(x):
    return 0.5 * x * (1.0 + jnp.tanh(np.sqrt(2.0 / np.pi) * (x + 0.044715 * (x * x * x))))


def _group_sumsq(v, ones_ref):
    width = ones_ref.shape[0]
    parts = []
    for c in range(v.shape[1] // width):
        blk = v[:, c * width:(c + 1) * width]
        parts.append(_dot((blk * blk).astype(BF16), ones_ref[...]))
    return parts[0] if len(parts) == 1 else jnp.concatenate(parts, axis=1)


def _rope(v, cos, sin, rot_ref):
    width = rot_ref.shape[0]
    parts = []
    for c in range(v.shape[1] // width):
        blk = v[:, c * width:(c + 1) * width]
        rot = _dot(blk.astype(BF16), rot_ref[...])
        for s in range(width // LANES):
            sl = slice(s * LANES, (s + 1) * LANES)
            parts.append(blk[:, sl] * cos + rot[:, sl] * sin)
    return parts[0] if len(parts) == 1 else jnp.concatenate(parts, axis=1)


def _modulated_norm(x, gain, shift, scale):
    ms = jnp.mean(x * x, axis=-1, keepdims=True)
    h = x * lax.rsqrt(ms + EPS) * gain
    return h * (1.0 + scale) + shift


def _adaln_kernel(cond_ref, w_ref, b_ref, o_ref):
    a = cond_ref[...]
    a = a * jax.nn.sigmoid(a)
    o_ref[...] = _dot(a.astype(BF16), w_ref[...].astype(BF16)) + b_ref[...]


def _adaln(cond8, w_mod, b_mod):
    depth, d, n = w_mod.shape
    tn = 1536
    return pl.pallas_call(
        _adaln_kernel,
        grid=(depth, n // tn),
        in_specs=[
            pl.BlockSpec((8, d), lambda l, j: (0, 0)),
            pl.BlockSpec((None, d, tn), lambda l, j: (l, 0, j)),
            pl.BlockSpec((None, 1, tn), lambda l, j: (l, 0, j)),
        ],
        out_specs=pl.BlockSpec((None, 8, tn), lambda l, j: (l, 0, j)),
        out_shape=jax.ShapeDtypeStruct((depth, 8, n), F32),
        compiler_params=_cparams("parallel", "parallel"),
        name="adaln",
    )(cond8, w_mod, b_mod.reshape(depth, 1, n))


def _mla_keys_values(ckv_n, kr_hi, wkn_ref, wv_ref, gk1_ref, gk2_ref, ones128_ref, rope_args):
    cb = ckv_n.astype(BF16)
    nope = _dot(cb, wkn_ref[...])
    vals = _dot(cb, wv_ref[...])
    kr_ss = jnp.sum(kr_hi * kr_hi, axis=-1, keepdims=True)
    inv = lax.rsqrt((_group_sumsq(nope, ones128_ref) + kr_ss) * (1.0 / MLA_QK) + EPS)
    krg = kr_hi * gk2_ref[...]
    if rope_args is not None:
        cos, sin, rot_ref = rope_args
        krg = _rope(krg, cos, sin, rot_ref)
    gk1 = gk1_ref[...]
    parts = []
    for h in range(MLA_HEADS):
        sl = slice(h * HEAD_PAD, (h + 1) * HEAD_PAD)
        parts.append((nope[:, sl] * gk1[:, sl] + krg) * inv[:, sl])
    return jnp.concatenate(parts, axis=1), vals


def _values_with_sum_rows(vals_pad, vone_ref):
    return (vals_pad + vone_ref[...]).T.astype(BF16)


def _ctx_keys_kernel(ckv_ref, kr_ref, wkn_ref, wv_ref, gk1_ref, gk2_ref, vone_ref, ones128_ref, k_ref, vt_ref):
    keys, vals = _mla_keys_values(ckv_ref[...], kr_ref[...], wkn_ref, wv_ref, gk1_ref, gk2_ref,
                                  ones128_ref, None)
    k_ref[...] = keys.astype(BF16)
    vt_ref[...] = _values_with_sum_rows(vals, vone_ref)


def _ctx_keys(cache_ckv, cache_kr_hi, wkn, wvp, gk1, gk2, vone, ones128):
    nb, depth, past, _ = cache_ckv.shape
    cache_map = lambda l, b: (b, l, 0, 0)
    w_map = lambda l, b: (l, 0, 0)
    return pl.pallas_call(
        _ctx_keys_kernel,
        grid=(depth, nb),
        in_specs=[
            pl.BlockSpec((None, None, past, MLA_KV_LORA), cache_map),
            pl.BlockSpec((None, None, past, LANES), cache_map),
            pl.BlockSpec((None, MLA_KV_LORA, MLA_PAD_COLS), w_map),
            pl.BlockSpec((None, MLA_KV_LORA, MLA_VT_ROWS), w_map),
            pl.BlockSpec((None, 1, MLA_PAD_COLS), w_map),
            pl.BlockSpec((None, 1, LANES), w_map),
            pl.BlockSpec((1, MLA_VT_ROWS), lambda l, b: (0, 0)),
            pl.BlockSpec((MXU_DIM, MXU_DIM), lambda l, b: (0, 0)),
        ],
        out_specs=[
            pl.BlockSpec((None, None, past, MLA_PAD_COLS), lambda l, b: (l, b, 0, 0)),
            pl.BlockSpec((None, None, MLA_VT_ROWS, past), lambda l, b: (l, b, 0, 0)),
        ],
        out_shape=[
            jax.ShapeDtypeStruct((depth, nb, past, MLA_PAD_COLS), BF16),
            jax.ShapeDtypeStruct((depth, nb, MLA_VT_ROWS, past), BF16),
        ],
        compiler_params=_cparams("parallel", "parallel"),
        name="ctx_keys",
    )(cache_ckv, cache_kr_hi, wkn, wvp, gk1, gk2, vone, ones128)


def _token_kernel(latent, x_ref, mod_ref, gmix_ref, wa_ref, gqa_ref, gka_ref, gcq_ref, wuq_ref, gq_ref,
                  gckv_ref, wkn_ref, wv_ref, gk1_ref, gk2_ref, lng_ref, lnb_ref, ws_ref, bs_ref,
                  ones64_ref, ones128_ref, *rest):
    if latent:
        (cosa_ref, sina_ref, cosb_ref, sinb_ref, rota_ref, rotb_ref, vone_ref,
         qa_o, kd_o, va_o, qb_o, kb_o, vb_o, oc_o) = rest
    else:
        (qa_o, kd_o, va_o, qb_o, kb_o, vb_o, oc_o, kac_o, vac_o, ckvc_o, krc_o) = rest
    d = D_MODEL
    rows = x_ref.shape[0]
    h = _modulated_norm(x_ref[...], gmix_ref[...], mod_ref[:, 0:d], mod_ref[:, d:2 * d])
    z = _dot(h.astype(BF16), wa_ref[...])

    qa = z[:, _QA0:_KA0]
    qa = qa * lax.rsqrt(_group_sumsq(qa, ones64_ref) * (1.0 / SWA_HEAD_DIM) + EPS) * gqa_ref[...]
    ka = z[:, _KA0:_VA0]
    ka_ss = _dot((ka * ka).astype(BF16), ones64_ref[0:LANES, 0:LANES])
    ka = ka * lax.rsqrt(ka_ss * (1.0 / SWA_HEAD_DIM) + EPS) * gka_ref[...]
    va = z[:, _VA0:_CQ0]
    if latent:
        qa = _rope(qa, cosa_ref[...], sina_ref[...], rota_ref)
        ka_r = _rope(ka, cosa_ref[...], sina_ref[...], rota_ref.at[0:LANES, 0:LANES])
    else:
        kac_o[...] = ka
        vac_o[...] = va
        ka_r = ka
    swapped = pltpu.roll(ka_r, SWA_HEAD_DIM, 1)
    low = lax.broadcasted_iota(jnp.int32, (rows, LANES), 1) < SWA_HEAD_DIM
    qa_o[...] = qa.astype(BF16)
    kd_o[:, 0:LANES] = jnp.where(low, ka_r, swapped).astype(BF16)
    kd_o[:, LANES:2 * LANES] = jnp.where(low, swapped, ka_r).astype(BF16)
    va_o[...] = va.T.astype(BF16) if latent else va.astype(BF16)

    cq = z[:, _CQ0:_CKV0]
    cq = cq * lax.rsqrt(jnp.mean(cq * cq, axis=-1, keepdims=True) + EPS) * gcq_ref[...]
    qb = _dot(cq.astype(BF16), wuq_ref[...])
    qb = qb * lax.rsqrt(_group_sumsq(qb, ones128_ref) * (1.0 / MLA_QK) + EPS) * gq_ref[...]
    if latent:
        qb = _rope(qb, cosb_ref[...], sinb_ref[...], rotb_ref)
    qb_o[...] = qb.astype(BF16)

    ckv = z[:, _CKV0:_KR0]
    ckv = ckv * lax.rsqrt(jnp.mean(ckv * ckv, axis=-1, keepdims=True) + EPS) * gckv_ref[...]
    krb = z[:, _KR0:_U0]
    lane = lax.broadcasted_iota(jnp.int32, (rows, LANES), 1)
    kr_hi = jnp.where(lane >= MLA_NOPE, krb, 0.0)
    rope_args = (cosb_ref[...], sinb_ref[...], rotb_ref.at[0:LANES, 0:LANES]) if latent else None
    kb, vb = _mla_keys_values(ckv, kr_hi, wkn_ref, wv_ref, gk1_ref, gk2_ref, ones128_ref, rope_args)
    kb_o[...] = kb.astype(BF16)
    if latent:
        vb_o[...] = _values_with_sum_rows(vb, vone_ref)
    else:
        vb_o[...] = vb.astype(BF16)
    if not latent:
        ckvc_o[...] = ckv
        krc_o[...] = krb[:, 0:MLA_ROPE]

    gu = _gelu_tanh(z[:, _U0:_V0])
    gv = _gelu_tanh(z[:, _V0:_ZCOLS])
    gc = gv - jnp.mean(gv, axis=-1, keepdims=True)
    var = jnp.mean(gc * gc, axis=-1, keepdims=True)
    vg = (gc * lax.rsqrt(var + EPS) * lng_ref[...] + lnb_ref[...]).astype(BF16)
    n_chunks = rows // GM_CHUNK
    for g in range(GM_GROUPS):
        gl = slice(g * LANES, (g + 1) * LANES)
        rhs = jnp.concatenate([vg[c * GM_CHUNK:(c + 1) * GM_CHUNK, gl] for c in range(n_chunks)], axis=1)
        mixed = _dot(ws_ref[g], rhs)
        for c in range(n_chunks):
            rs = slice(c * GM_CHUNK, (c + 1) * GM_CHUNK)
            oc_o[rs, gl] = (gu[rs, gl] * (mixed[:, c * LANES:(c + 1) * LANES] + bs_ref[:, gl])).astype(BF16)


def _token_stage(latent, x, n_tiles, mod_l, mod_row, wl, consts, tables):
    t = TOKEN_TILE
    d = D_MODEL
    n = n_tiles * t
    full = lambda a: pl.BlockSpec(a.shape, lambda i: (0,) * a.ndim)
    weights = [wl["g_mix"], wl["w_a"], wl["gqa"], wl["gka"], wl["gcq"], wl["w_uq"], wl["gq"], wl["gckv"],
               wl["w_kn"], wl["w_vp"] if latent else wl["w_v"], wl["gk1"], wl["gk2"], wl["ln_g"], wl["ln_b"], wl["w_s"], wl["b_s"],
               consts["ones64"], consts["ones128"]]
    in_specs = [
        pl.BlockSpec((t, d), lambda i: (i, 0)),
        pl.BlockSpec((None, 1, 6 * d), lambda i: (mod_row(i), 0, 0)),
    ] + [full(a) for a in weights]
    args = [x, mod_l] + weights
    row = lambda w: pl.BlockSpec((t, w), lambda i: (i, 0))
    out_widths = [SWA_Q_COLS, 2 * LANES, LANES, MLA_PAD_COLS, MLA_PAD_COLS, MLA_HEADS * MLA_V, GM_WIDTH]
    out_specs = [row(w) for w in out_widths]
    out_shape = [jax.ShapeDtypeStruct((n, w), BF16) for w in out_widths]
    if latent:
        tiles_per_seq = tables["cos_a"].shape[0] // t
        tab = lambda: pl.BlockSpec((t, LANES), lambda i: (i % tiles_per_seq, 0))
        in_specs += [tab(), tab(), tab(), tab(), full(consts["rot_a"]), full(consts["rot_b"]),
                     full(consts["v_ones"])]
        args += [tables["cos_a"], tables["sin_a"], tables["cos_b"], tables["sin_b"],
                 consts["rot_a"], consts["rot_b"], consts["v_ones"]]
        out_specs[5] = pl.BlockSpec((MLA_VT_ROWS, t), lambda i: (0, i))
        out_shape[5] = jax.ShapeDtypeStruct((MLA_VT_ROWS, n), BF16)
        out_specs[2] = pl.BlockSpec((LANES, t), lambda i: (0, i))
        out_shape[2] = jax.ShapeDtypeStruct((LANES, n), BF16)
    else:
        cache_widths = [SWA_KV_COLS, SWA_KV_COLS, MLA_KV_LORA, MLA_ROPE]
        out_specs += [row(w) for w in cache_widths]
        out_shape += [jax.ShapeDtypeStruct((n, w), F32) for w in cache_widths]
    return pl.pallas_call(
        functools.partial(_token_kernel, latent),
        grid=(n_tiles,),
        in_specs=in_specs,
        out_specs=out_specs,
        out_shape=out_shape,
        compiler_params=_cparams("parallel"),
        name="token_stage_latent" if latent else "token_stage_ctx",
    )(*args)


def _softmax_pv(scores, values, extra=None):
    m = jnp.max(scores[0], axis=-1, keepdims=True)
    for s in scores[1:]:
        m = jnp.maximum(m, jnp.max(s, axis=-1, keepdims=True))
    if extra is not None:
        m = jnp.maximum(m, extra)
    den = jnp.exp(extra - m) if extra is not None else 0.0
    acc = None
    for s, v in zip(scores, values):
        p = jnp.exp(s - m)
        den = den + jnp.sum(p, axis=-1, keepdims=True)
        pv = _dot(p.astype(BF16), v)
        acc = pv if acc is None else acc + pv
    return acc / den


def _ctx_attn_kernel(sink_ref, qa_ref, kd_ref, va_ref, qb_ref, kb_ref, vb_ref, oa_ref, ob_ref):
    rows = qa_ref.shape[0]
    low = lax.broadcasted_iota(jnp.int32, (rows, LANES), 1) < SWA_HEAD_DIM
    va = va_ref[...]
    res = []
    for hd in range(SWA_HEADS):
        qblk = qa_ref[:, (hd // 2) * LANES:(hd // 2 + 1) * LANES]
        keep = low if hd % 2 == 0 else jnp.logical_not(low)
        qm = jnp.where(keep, qblk, jnp.zeros_like(qblk))
        kvh = hd // SWA_GROUP
        s = _nt_dot(qm, kd_ref[:, kvh * LANES:(kvh + 1) * LANES]) * (SWA_HEAD_DIM ** -0.5)
        res.append(_softmax_pv([s], [va], sink_ref[hd]))
    for j in range(SWA_GROUP):
        oa_ref[:, j * LANES:(j + 1) * LANES] = jnp.where(low, res[j], res[SWA_GROUP + j]).astype(BF16)
    res = []
    for h in range(MLA_HEADS):
        sl = slice(h * HEAD_PAD, (h + 1) * HEAD_PAD)
        s = _nt_dot(qb_ref[:, sl], kb_ref[:, sl]) * (MLA_QK ** -0.5)
        res.append(_softmax_pv([s], [vb_ref[:, (h // 2) * LANES:(h // 2 + 1) * LANES]]))
    for j in range(MLA_HEADS // 2):
        ob_ref[:, j * LANES:(j + 1) * LANES] = jnp.where(low, res[2 * j], res[2 * j + 1]).astype(BF16)


def _ctx_attention(sink, qa, kd, va, qb, kb, vb, n_seq, seq):
    row = lambda w: pl.BlockSpec((seq, w), lambda b: (b, 0))
    widths = [SWA_Q_COLS, 2 * LANES, LANES, MLA_PAD_COLS, MLA_PAD_COLS, MLA_HEADS * MLA_V]
    return pl.pallas_call(
        _ctx_attn_kernel,
        grid=(n_seq,),
        in_specs=[pl.BlockSpec(memory_space=pltpu.SMEM)] + [row(w) for w in widths],
        out_specs=[row(SWA_Q_COLS), row(MLA_HEADS * MLA_V)],
        out_shape=[jax.ShapeDtypeStruct((n_seq * seq, SWA_Q_COLS), BF16),
                   jax.ShapeDtypeStruct((n_seq * seq, MLA_HEADS * MLA_V), BF16)],
        compiler_params=_cparams("parallel"),
        name="ctx_attention",
    )(sink, qa, kd, va, qb, kb, vb)


def _window_attn_kernel(sink_ref, q_ref, kp_ref, kc_ref, kn_ref, vtp_ref, vtc_ref, vtn_ref, kctx_ref,
                        vtctx_ref, o_ref):
    w = SWA_WINDOW
    i = pl.program_id(1)
    nb = pl.num_programs(1)
    cols = SWA_GROUP * w
    c = (SWA_HEAD_DIM ** -0.5) * LOG2_E
    low = lax.broadcasted_iota(jnp.int32, (w, LANES), 1) < SWA_HEAD_DIM
    m_io = lax.broadcasted_iota(jnp.int32, (3 * w, cols), 0)
    rel = m_io - (lax.broadcasted_iota(jnp.int32, (3 * w, cols), 1) & (w - 1))
    first_row = jnp.where(i == 0, w, 0)
    end_row = jnp.where(i == nb - 1, 2 * w, 3 * w)
    valid = (rel >= 0) & (rel <= 2 * w) & (m_io >= first_row) & (m_io < end_row)
    head_of_col = lax.broadcasted_iota(jnp.int32, (1, cols), 1) >> (w.bit_length() - 1)
    vt_loc = jnp.concatenate([vtp_ref[...], vtc_ref[...], vtn_ref[...]], axis=1)
    vt_ctx = vtctx_ref[...]
    top_loc = lax.broadcasted_iota(jnp.int32, vt_loc.shape, 0) < SWA_HEAD_DIM
    top_ctx = lax.broadcasted_iota(jnp.int32, vt_ctx.shape, 0) < SWA_HEAD_DIM
    staged = []
    for kvh in range(SWA_KV_HEADS):
        qs = []
        sink = jnp.zeros((1, cols), F32)
        for g in range(SWA_GROUP):
            hd = kvh * SWA_GROUP + g
            qblk = q_ref[:, (hd // 2) * LANES:(hd // 2 + 1) * LANES]
            keep = low if hd % 2 == 0 else jnp.logical_not(low)
            qs.append(jnp.where(keep, qblk, jnp.zeros_like(qblk)))
            sink = jnp.where(head_of_col == g, sink_ref[hd], sink)
        qs = jnp.concatenate(qs, axis=0)
        ks = slice(kvh * LANES, (kvh + 1) * LANES)
        k_loc = jnp.concatenate([kp_ref[:, ks], kc_ref[:, ks], kn_ref[:, ks]], axis=0)
        st_ctx = _nt_dot(kctx_ref[:, ks], qs)
        st_loc = jnp.where(valid, _nt_dot(k_loc, qs), NEG_INF)
        staged.append((st_ctx, st_loc, sink))
    res = []
    for kvh, (st_ctx, st_loc, sink) in enumerate(staged):
        top = jnp.maximum(jnp.max(st_ctx, axis=0, keepdims=True), jnp.max(st_loc, axis=0, keepdims=True))
        m2 = jnp.maximum(top * (SWA_HEAD_DIM ** -0.5), sink) * LOG2_E
        p_ctx = jnp.exp2(st_ctx * c - m2).astype(BF16)
        p_loc = jnp.exp2(st_loc * c - m2).astype(BF16)
        own_loc = top_loc if kvh == 0 else jnp.logical_not(top_loc)
        own_ctx = top_ctx if kvh == 0 else jnp.logical_not(top_ctx)
        acc = (_dot(jnp.where(own_ctx, vt_ctx, jnp.ones_like(vt_ctx)), p_ctx)
               + _dot(jnp.where(own_loc, vt_loc, jnp.ones_like(vt_loc)), p_loc))
        own = slice(kvh * SWA_HEAD_DIM, (kvh + 1) * SWA_HEAD_DIM)
        other = (1 - kvh) * SWA_HEAD_DIM
        den = acc[other:other + 1] + jnp.exp2(sink * LOG2_E - m2)
        res.append(acc[own] / den)
    for j in range(SWA_GROUP):
        cs = slice(j * w, (j + 1) * w)
        o_ref[:, j * LANES:(j + 1) * LANES] = jnp.concatenate([res[0][:, cs], res[1][:, cs]], axis=0).T.astype(BF16)


def _window_attention(sink, qa, kd, vat, kd_ctx, vt_ctx, n_seq, seq):
    w = SWA_WINDOW
    nb = seq // w
    past = kd_ctx.shape[1]
    cur = lambda b, i: (b * nb + i, 0)
    prev = lambda b, i: (b * nb + jnp.maximum(i - 1, 0), 0)
    nxt = lambda b, i: (b * nb + jnp.minimum(i + 1, nb - 1), 0)
    flip = lambda f: (lambda b, i: f(b, i)[::-1])
    return pl.pallas_call(
        _window_attn_kernel,
        grid=(n_seq, nb),
        in_specs=[
            pl.BlockSpec(memory_space=pltpu.SMEM),
            pl.BlockSpec((w, SWA_Q_COLS), cur),
            pl.BlockSpec((w, 2 * LANES), prev),
            pl.BlockSpec((w, 2 * LANES), cur),
            pl.BlockSpec((w, 2 * LANES), nxt),
            pl.BlockSpec((LANES, w), flip(prev)),
            pl.BlockSpec((LANES, w), flip(cur)),
            pl.BlockSpec((LANES, w), flip(nxt)),
            pl.BlockSpec((None, past, 2 * LANES), lambda b, i: (b, 0, 0)),
            pl.BlockSpec((None, LANES, past), lambda b, i: (b, 0, 0)),
        ],
        out_specs=pl.BlockSpec((w, SWA_Q_COLS), cur),
        out_shape=jax.ShapeDtypeStruct((n_seq * seq, SWA_Q_COLS), BF16),
        compiler_params=_cparams("parallel", "parallel"),
        name="window_attention",
    )(sink, qa, kd, kd, kd, vat, vat, vat, kd_ctx, vt_ctx)


def _latent_mla_kernel(q_ref, k_ref, vt_ref, kctx_ref, vtctx_ref, o_ref):
    kc = MLA_KEY_CHUNK
    c = (MLA_QK ** -0.5) * LOG2_E
    chunks = ([(kctx_ref, vtctx_ref, j) for j in range(kctx_ref.shape[0] // kc)]
              + [(k_ref, vt_ref, j) for j in range(k_ref.shape[0] // kc)])
    heads = [slice(e * HEAD_PAD, (e + 1) * HEAD_PAD) for e in range(2)]

    def scores(n):
        keys_ref, _, j = chunks[n]
        return [_nt_dot(keys_ref[j * kc:(j + 1) * kc, sl], q_ref[:, sl]) for sl in heads]

    m = [None, None]
    acc = [None, None]
    pending = [scores(n) for n in range(min(MLA_SCORE_LOOKAHEAD, len(chunks)))]
    for n, (_, vals_ref, j) in enumerate(chunks):
        if n + MLA_SCORE_LOOKAHEAD < len(chunks):
            pending.append(scores(n + MLA_SCORE_LOOKAHEAD))
        st = pending.pop(0)
        for e, sl in enumerate(heads):
            cmax = jnp.max(st[e], axis=0, keepdims=True)
            vals = vals_ref[e * MLA_V_ROWS:(e + 1) * MLA_V_ROWS, j * kc:(j + 1) * kc]
            if n == 0:
                m[e] = cmax
                acc[e] = _dot(vals, jnp.exp2((st[e] - cmax) * c).astype(BF16))
            else:
                m_new = jnp.maximum(m[e], cmax)
                p = jnp.exp2((st[e] - m_new) * c).astype(BF16)
                acc[e] = acc[e] * jnp.exp2((m[e] - m_new) * c) + _dot(vals, p)
                m[e] = m_new
    outs = [a[0:MLA_V] / a[MLA_V:MLA_V + 1] for a in acc]
    o_ref[...] = jnp.concatenate(outs, axis=0).T.astype(BF16)


def _latent_mla(qb, kb, vb, k_ctx, v_ctx, n_seq, seq):
    tq = MLA_Q_TILE
    nq = seq // tq
    past = k_ctx.shape[1]
    pair = 2 * HEAD_PAD
    assert past % MLA_KEY_CHUNK == 0 and seq % MLA_KEY_CHUNK == 0
    return pl.pallas_call(
        _latent_mla_kernel,
        grid=(n_seq, MLA_HEADS // 2, nq),
        in_specs=[
            pl.BlockSpec((tq, pair), lambda b, p, i: (b * nq + i, p)),
            pl.BlockSpec((seq, pair), lambda b, p, i: (b, p)),
            pl.BlockSpec((2 * MLA_V_ROWS, seq), lambda b, p, i: (p, b)),
            pl.BlockSpec((None, past, pair), lambda b, p, i: (b, 0, p)),
            pl.BlockSpec((None, 2 * MLA_V_ROWS, past), lambda b, p, i: (b, p, 0)),
        ],
        out_specs=pl.BlockSpec((tq, LANES), lambda b, p, i: (b * nq + i, p)),
        out_shape=jax.ShapeDtypeStruct((n_seq * seq, MLA_HEADS * MLA_V), BF16),
        compiler_params=_cparams("parallel", "parallel", "parallel"),
        name="latent_mla",
    )(qb, kb, vb, k_ctx, v_ctx)


def _route(logits):
    lane = lax.broadcasted_iota(jnp.int32, logits.shape, 1).astype(F32)
    big = float(4 * LANES)
    gmask = lane < N_EXPERT_GROUPS
    gmax = jnp.max(jnp.where(gmask, logits, -jnp.inf), axis=-1, keepdims=True)
    gsum = jnp.sum(jnp.where(gmask, jnp.exp(logits - gmax), 0.0), axis=-1, keepdims=True)
    g_w = 1.0 / gsum
    g_idx = jnp.min(jnp.where(gmask & (logits == gmax), lane, big), axis=-1, keepdims=True)
    first = ROUTER_LANE0 + EXPERTS_PER_GROUP * g_idx
    emask = (lane >= first) & (lane < first + EXPERTS_PER_GROUP)
    el = jnp.where(emask, logits, -jnp.inf)
    m1 = jnp.max(el, axis=-1, keepdims=True)
    i1 = jnp.min(jnp.where(emask & (el == m1), lane, big), axis=-1, keepdims=True)
    el2 = jnp.where(lane == i1, -jnp.inf, el)
    m2 = jnp.max(el2, axis=-1, keepdims=True)
    i2 = jnp.min(jnp.where(emask & (el2 == m2), lane, big), axis=-1, keepdims=True)
    esum = jnp.sum(jnp.where(emask, jnp.exp(logits - m1), 0.0), axis=-1, keepdims=True)
    p1 = 1.0 / esum
    p2 = jnp.exp(m2 - m1) / esum
    tot = p1 + p2
    combine = jnp.where(lane == i1, g_w * (p1 / tot), 0.0) + jnp.where(lane == i2, g_w * (p2 / tot), 0.0)
    return jnp.where(lane == g_idx, 1.0, combine)


def _finish_kernel(x_ref, oa_ref, ob_ref, oc_ref, mod_ref, gmix_ref, gffn_ref, wg_ref, woa_ref, wob_ref,
                   woc_ref, wout_ref, wrh_ref, wrl_ref, br_ref, x1_ref, h2_ref, comb_ref):
    d = D_MODEL
    x = x_ref[...]
    h = _modulated_norm(x, gmix_ref[...], mod_ref[:, 0:d], mod_ref[:, d:2 * d]).astype(BF16)
    gates = _dot(h, wg_ref[...])
    merged = (jax.nn.sigmoid(gates[:, 0:d]) * _dot(oa_ref[...], woa_ref[...])
              + jax.nn.sigmoid(gates[:, d:2 * d]) * _dot(ob_ref[...], wob_ref[...])
              + jax.nn.sigmoid(gates[:, 2 * d:3 * d]) * _dot(oc_ref[...], woc_ref[...]))
    x1 = x + mod_ref[:, 2 * d:3 * d] * _dot(merged.astype(BF16), wout_ref[...])
    x1_ref[...] = x1
    h2 = _modulated_norm(x1, gffn_ref[...], mod_ref[:, 3 * d:4 * d], mod_ref[:, 4 * d:5 * d])
    h2_hi = h2.astype(BF16)
    h2_lo = (h2 - h2_hi.astype(F32)).astype(BF16)
    h2_ref[...] = h2_hi
    logits = (_dot(h2_hi, wrh_ref[...]) + _dot(h2_lo, wrh_ref[...]) + _dot(h2_hi, wrl_ref[...])) + br_ref[...]
    comb_ref[...] = _route(logits)


def _finish(x_all, oa, ob, oc, mod_l, mod_row, wl):
    t = TOKEN_TILE
    d = D_MODEL
    n = x_all.shape[0]
    full = lambda a: pl.BlockSpec(a.shape, lambda i: (0,) * a.ndim)
    row = lambda w: pl.BlockSpec((t, w), lambda i: (i, 0))
    weights = [wl["g_mix"], wl["g_ffn"], wl["w_g"], wl["w_o_a"], wl["w_o_b"], wl["w_o_c"], wl["w_out"],
               wl["w_r_hi"], wl["w_r_lo"], wl["b_r"]]
    return pl.pallas_call(
        _finish_kernel,
        grid=(n // t,),
        in_specs=[row(d), row(SWA_Q_COLS), row(MLA_HEADS * MLA_V), row(GM_WIDTH),
                  pl.BlockSpec((None, 1, 6 * d), lambda i: (mod_row(i), 0, 0))] + [full(a) for a in weights],
        out_specs=[row(d), row(d), row(LANES)],
        out_shape=[jax.ShapeDtypeStruct((n, d), F32), jax.ShapeDtypeStruct((n, d), BF16),
                   jax.ShapeDtypeStruct((n, LANES), F32)],
        compiler_params=_cparams("parallel"),
        name="finish",
    )(x_all, oa, ob, oc, mod_l, *weights)


def _moe_kernel(h_ref, route_ref, x1_ref, mod_ref, tri_ref, wg_ref, wu_ref, wd_ref, o_ref,
                route_t, rank_rows, rank_cols, route_split, xg, yacc, wsel):
    d = D_MODEL
    t = h_ref.shape[0]
    ch = MOE_CHUNK
    s = pl.program_id(1)
    steps_per_group = EXPERTS_PER_GROUP // MOE_EXPERTS_PER_STEP
    g = s // steps_per_group
    j0 = (s % steps_per_group) * MOE_EXPERTS_PER_STEP

    @pl.when(s == 0)
    def _():
        rt = route_ref[...].T
        route_t[...] = rt
        rr = _dot(rt.astype(BF16), tri_ref[...])
        rank_rows[...] = rr
        rank_cols[...] = rr.T
        route = route_ref[...]
        hi = route.astype(BF16)
        route_split[:, 0:LANES] = hi
        route_split[:, LANES:2 * LANES] = (route - hi.astype(F32)).astype(BF16)
        o_ref[...] = jnp.zeros_like(o_ref)

    ind_row = route_t[pl.ds(g, 1), :]
    n_rows = jnp.sum(ind_row).astype(jnp.int32)
    n_chunks = (n_rows + (ch - 1)) // ch

    def chunk_rows(c):
        return pl.ds(pl.multiple_of(c * ch, ch), ch)

    @pl.when(j0 == 0)
    def _():
        rank_row = rank_rows[pl.ds(g, 1), :]
        member = ind_row > 0.5
        r_io = lax.broadcasted_iota(jnp.int32, (ch, t), 0).astype(F32)

        def gather(c, carry):
            sel = (rank_row == r_io + (c * ch).astype(F32)) & member
            sel = jnp.where(sel, 1.0, 0.0).astype(BF16)
            rows = chunk_rows(c)
            xg[rows, :] = _dot(sel, h_ref[...]).astype(BF16)
            w2 = _dot(sel, route_split[...])
            wsel[rows, :] = w2[:, 0:LANES] + w2[:, LANES:2 * LANES]
            yacc[rows, :] = jnp.zeros((ch, d), F32)
            return carry

        lax.fori_loop(0, n_chunks, gather, 0)

    def ffn(c, carry):
        rows = chunk_rows(c)
        x = xg[rows, :]
        lane = lax.broadcasted_iota(jnp.int32, (ch, LANES), 1)
        wt = wsel[rows, :]
        total = None
        for k in range(MOE_EXPERTS_PER_STEP):
            a = _dot(x, wg_ref[k])
            hid = (a * jax.nn.sigmoid(a)) * _dot(x, wu_ref[k])
            y = _dot(hid.astype(BF16), wd_ref[k])
            w_col = jnp.sum(jnp.where(lane == ROUTER_LANE0 + EXPERTS_PER_GROUP * g + j0 + k, wt, 0.0), axis=-1,
                            keepdims=True)
            total = w_col * y if total is None else total + w_col * y
        yacc[rows, :] += total
        return carry

    lax.fori_loop(0, n_chunks, ffn, 0)

    @pl.when(j0 == EXPERTS_PER_GROUP - MOE_EXPERTS_PER_STEP)
    def _():
        lane_t = lax.broadcasted_iota(jnp.int32, (t, LANES), 1)
        member = jnp.sum(jnp.where(lane_t == g, route_ref[...], 0.0), axis=-1, keepdims=True) > 0.5
        rank_col = jnp.sum(jnp.where(lane_t == g, rank_cols[...], 0.0), axis=-1, keepdims=True)
        c_io = lax.broadcasted_iota(jnp.int32, (t, ch), 1).astype(F32)

        def scatter(c, carry):
            sel_t = (rank_col == c_io + (c * ch).astype(F32)) & member
            o_ref[...] += _dot(jnp.where(sel_t, 1.0, 0.0).astype(BF16), yacc[chunk_rows(c), :].astype(BF16))
            return carry

        lax.fori_loop(0, n_chunks, scatter, 0)

    @pl.when(s == pl.num_programs(1) - 1)
    def _():
        o_ref[...] = x1_ref[...] + mod_ref[:, 5 * d:6 * d] * o_ref[...]


def _moe(h2, route, x1, mod_l, mod_row, wl, tri):
    d = D_MODEL
    n = h2.shape[0]
    tile = tri.shape[0]
    chunk_rows = pl.cdiv(tile, MOE_CHUNK) * MOE_CHUNK
    eps = MOE_EXPERTS_PER_STEP
    row = lambda w: pl.BlockSpec((tile, w), lambda i, e: (i, 0))
    return pl.pallas_call(
        _moe_kernel,
        grid=(n // tile, N_EXPERTS // eps),
        in_specs=[row(d), row(LANES), row(d),
                  pl.BlockSpec((None, 1, 6 * d), lambda i, e: (mod_row(i), 0, 0)),
                  pl.BlockSpec((tile, tile), lambda i, e: (0, 0)),
                  pl.BlockSpec((eps, d, EXPERT_FF), lambda i, e: (e, 0, 0)),
                  pl.BlockSpec((eps, d, EXPERT_FF), lambda i, e: (e, 0, 0)),
                  pl.BlockSpec((eps, EXPERT_FF, d), lambda i, e: (e, 0, 0))],
        out_specs=row(d),
        out_shape=jax.ShapeDtypeStruct((n, d), F32),
        scratch_shapes=[pltpu.VMEM((LANES, tile), F32), pltpu.VMEM((LANES, tile), F32),
                        pltpu.VMEM((tile, LANES), F32), pltpu.VMEM((tile, 2 * LANES), BF16),
                        pltpu.VMEM((chunk_rows, d), BF16),
                        pltpu.VMEM((chunk_rows, d), F32), pltpu.VMEM((chunk_rows, LANES), F32)],
        compiler_params=_cparams("parallel", "arbitrary"),
        name="moe_grouped",
    )(h2, route, x1, mod_l, tri, wl["w_gate"], wl["w_up"], wl["w_down"])


def _block_ones(group):
    idx = np.arange(MXU_DIM)
    return jnp.asarray((idx[:, None] // group) == (idx[None, :] // group), BF16)


def _rotate_half_matrix(block, rot_start, rot_dim):
    half = rot_dim // 4
    r = np.zeros((MXU_DIM, MXU_DIM), np.float32)
    for j in range(MXU_DIM):
        o = j % block - rot_start
        if 0 <= o < rot_dim:
            partner = j + half if (o % (2 * half)) < half else j - half
            r[partner, j] = 1.0
    return jnp.asarray(r, BF16)


def _rope_tables(n_tokens, block, rot_start, rot_dim):
    half = rot_dim // 4
    tok = np.arange(n_tokens)
    pos = np.stack([tok // GRID_W, tok % GRID_W], axis=1).astype(np.float32)
    inv_freq = (ROPE_THETA ** (-np.arange(half, dtype=np.float32) / half)).astype(np.float32)
    lane = np.arange(LANES)
    o = lane % block - rot_start
    rot = (o >= 0) & (o < rot_dim)
    oc = np.where(rot, o, 0)
    axis = oc // (2 * half)
    freq = inv_freq[oc % half]
    sign = np.where((oc % (2 * half)) < half, -1.0, 1.0).astype(np.float32)
    ang = jnp.asarray(pos[:, axis]) * jnp.asarray(freq)[None, :]
    rot_j = jnp.asarray(rot)[None, :]
    cos = jnp.where(rot_j, jnp.cos(ang), 1.0)
    sin = jnp.where(rot_j, jnp.sin(ang) * jnp.asarray(sign)[None, :], 0.0)
    return cos.astype(F32), sin.astype(F32)


def _pad_heads(w, heads, width, padded=HEAD_PAD):
    lead = w.shape[:-1]
    w = w.reshape(lead + (heads, width))
    w = jnp.pad(w, [(0, 0)] * len(lead) + [(0, 0), (0, padded - width)])
    return w.reshape(lead + (heads * padded,))


def _prep_weights(p):
    depth = p["w_in"].shape[0]
    w_in = p["w_in"]
    c = np.cumsum((0, SWA_Q_COLS, SWA_KV_COLS, SWA_KV_COLS, MLA_Q_LORA, MLA_KV_LORA, MLA_ROPE, GM_WIDTH,
                   GM_WIDTH))
    kr = w_in[..., c[5]:c[6]]
    zero = jnp.zeros_like(kr)
    w_a = jnp.concatenate([w_in[..., c[0]:c[5]], kr, zero, kr, zero, w_in[..., c[6]:c[8]]], axis=-1)
    ukv = p["mla_w_ukv"].reshape(depth, MLA_KV_LORA, MLA_HEADS, MLA_NOPE + MLA_V)
    order = np.array([0, 4, 1, 5, 2, 6, 3, 7])
    w_o_a = p["w_o_a"].reshape(depth, SWA_HEADS, SWA_HEAD_DIM, D_MODEL)[:, order].reshape(depth, SWA_Q_COLS,
                                                                                          D_MODEL)
    w_r = jnp.concatenate([p["w_rg"], p["w_re"]], axis=-1)
    w_r = jnp.pad(w_r, ((0, 0), (0, 0), (0, LANES - w_r.shape[-1])))
    w_r_hi = w_r.astype(BF16)
    b_r = jnp.concatenate([p["b_rg"], p["b_re"]], axis=-1)
    b_r = jnp.pad(b_r, ((0, 0), (0, LANES - b_r.shape[-1])))
    k_gain = p["mla_k_norm"]
    row = lambda a: a.reshape(depth, 1, a.shape[-1]).astype(F32)
    bs_full = jnp.repeat(jnp.swapaxes(p["gm_b_s"], 1, 2), LANES, axis=-1)
    return {
        "g_mix": row(p["g_mix"]), "g_ffn": row(p["g_ffn"]),
        "w_a": w_a.astype(BF16), "w_g": w_in[..., _GATE0:].astype(BF16),
        "gqa": row(jnp.tile(p["swa_q_norm"], (1, SWA_HEADS))),
        "gka": row(jnp.tile(p["swa_k_norm"], (1, SWA_KV_HEADS))),
        "gcq": row(p["mla_cq_norm"]), "gckv": row(p["mla_ckv_norm"]),
        "w_uq": _pad_heads(p["mla_w_uq"], MLA_HEADS, MLA_QK).astype(BF16),
        "gq": row(_pad_heads(jnp.tile(p["mla_q_norm"], (1, MLA_HEADS)), MLA_HEADS, MLA_QK)),
        "w_kn": _pad_heads(ukv[..., :MLA_NOPE].reshape(depth, MLA_KV_LORA, -1), MLA_HEADS, MLA_NOPE).astype(BF16),
        "w_v": ukv[..., MLA_NOPE:].reshape(depth, MLA_KV_LORA, -1).astype(BF16),
        "w_vp": _pad_heads(ukv[..., MLA_NOPE:].reshape(depth, MLA_KV_LORA, -1), MLA_HEADS, MLA_V,
                           MLA_V_ROWS).astype(BF16),
        "gk1": row(_pad_heads(jnp.tile(k_gain[:, :MLA_NOPE], (1, MLA_HEADS)), MLA_HEADS, MLA_NOPE)),
        "gk2": row(jnp.pad(k_gain[:, MLA_NOPE:], ((0, 0), (MLA_NOPE, LANES - MLA_QK)))),
        "ln_g": row(p["gm_ln_g"]), "ln_b": row(p["gm_ln_b"]),
        "w_s": p["gm_w_s"].astype(BF16), "b_s": bs_full.astype(F32),
        "w_o_a": w_o_a.astype(BF16), "w_o_b": p["w_o_b"].astype(BF16), "w_o_c": p["w_o_c"].astype(BF16),
        "w_out": p["w_out"].astype(BF16),
        "w_r_hi": w_r_hi, "w_r_lo": (w_r - w_r_hi.astype(F32)).astype(BF16), "b_r": row(b_r),
        "w_gate": p["w_gate"].astype(BF16), "w_up": p["w_up"].astype(BF16), "w_down": p["w_down"].astype(BF16),
    }


def kernel(x_prompt, x_sample, cache_swa_k, cache_swa_v, cache_mla_ckv, cache_mla_krope, c, c_ctx, w_mod, b_mod, g_mix, g_ffn, w_in, swa_q_norm, swa_k_norm, swa_sink, mla_cq_norm, mla_ckv_norm, mla_w_uq, mla_w_ukv, mla_q_norm, mla_k_norm, gm_ln_g, gm_ln_b, gm_w_s, gm_b_s, w_o_a, w_o_b, w_o_c, w_out, w_rg, b_rg, w_re, b_re, w_gate, w_up, w_down):
    d = D_MODEL
    n_ctx_seq, ctx_len, _ = x_prompt.shape
    n_lat_seq, lat_len, _ = x_sample.shape
    depth = w_in.shape[0]
    past = cache_swa_k.shape[2]
    n_ctx = n_ctx_seq * ctx_len
    n_lat = n_lat_seq * lat_len
    t = TOKEN_TILE
    assert n_ctx % t == 0 and ctx_len % GM_CHUNK == 0 and lat_len % t == 0 and lat_len % MLA_Q_TILE == 0 and lat_len % GRID_W == 0 and n_lat_seq < 8
    ctx_tiles = n_ctx // t
    lat_tiles = n_lat // t
    moe_tile = next(m for m in MOE_TILE_CANDIDATES if n_ctx % m == 0 and lat_len % m == 0)

    params = dict(w_in=w_in, g_mix=g_mix, g_ffn=g_ffn, swa_q_norm=swa_q_norm, swa_k_norm=swa_k_norm,
                  mla_cq_norm=mla_cq_norm, mla_ckv_norm=mla_ckv_norm, mla_w_uq=mla_w_uq, mla_w_ukv=mla_w_ukv,
                  mla_q_norm=mla_q_norm, mla_k_norm=mla_k_norm, gm_ln_g=gm_ln_g, gm_ln_b=gm_ln_b,
                  gm_w_s=gm_w_s, gm_b_s=gm_b_s, w_o_a=w_o_a, w_o_b=w_o_b, w_o_c=w_o_c, w_out=w_out,
                  w_rg=w_rg, b_rg=b_rg, w_re=w_re, b_re=b_re, w_gate=w_gate, w_up=w_up, w_down=w_down)
    w_all = _prep_weights(params)
    consts = {
        "ones64": _block_ones(SWA_HEAD_DIM), "ones128": _block_ones(HEAD_PAD),
        "rot_a": _rotate_half_matrix(SWA_HEAD_DIM, 0, SWA_HEAD_DIM),
        "rot_b": _rotate_half_matrix(HEAD_PAD, MLA_NOPE, MLA_ROPE),
        "v_ones": jnp.asarray((np.arange(MLA_VT_ROWS) % MLA_V_ROWS >= MLA_V).astype(np.float32)[None, :]),
        "tri": jnp.asarray(np.triu(np.ones((moe_tile, moe_tile), np.float32), k=1), BF16),
    }
    cos_a, sin_a = _rope_tables(lat_len, SWA_HEAD_DIM, 0, SWA_HEAD_DIM)
    cos_b, sin_b = _rope_tables(lat_len, HEAD_PAD, MLA_NOPE, MLA_ROPE)
    tables = {"cos_a": cos_a, "sin_a": sin_a, "cos_b": cos_b, "sin_b": sin_b}

    cond8 = jnp.zeros((8, d), F32).at[:n_lat_seq].set(c).at[n_lat_seq].set(c_ctx)
    mods = _adaln(cond8, w_mod, b_mod).reshape(depth, 8, 1, 6 * d)

    k0 = cache_swa_k[:, :, :, 0, :]
    k1 = cache_swa_k[:, :, :, 1, :]
    kd_ctx_all = jnp.concatenate([k0, k0, k1, k1], axis=-1).astype(BF16)
    vt_swa_ctx_all = jnp.swapaxes(cache_swa_v.reshape(n_lat_seq, depth, past, SWA_KV_COLS), 2, 3).astype(BF16)
    kr_hi = jnp.pad(cache_mla_krope, ((0, 0), (0, 0), (0, 0), (MLA_NOPE, LANES - MLA_QK)))
    kb_ctx_all, vt_ctx_all = _ctx_keys(cache_mla_ckv, kr_hi, w_all["w_kn"], w_all["w_vp"], w_all["gk1"],
                                       w_all["gk2"], consts["v_ones"], consts["ones128"])

    ctx_row = lambda i: n_lat_seq
    lat_row = lambda i: i // (lat_len // t)
    lat_moe_row = lambda i: i // (lat_len // moe_tile)

    x_c = x_prompt.reshape(n_ctx, d)
    x_s = x_sample.reshape(n_lat, d)
    caches = [[], [], [], []]
    for l in range(depth):
        wl = {k: v[l] for k, v in w_all.items()}
        mod_l = mods[l]
        sink = swa_sink[l].reshape(SWA_HEADS)
        (qa_c, kd_c, va_c, qb_c, kb_c, vb_c, oc_c, ka_f, va_f, ckv_f, kr_f) = _token_stage(
            False, x_c, ctx_tiles, mod_l, ctx_row, wl, consts, None)
        (qa_s, kd_s, va_s, qb_s, kb_s, vb_s, oc_s) = _token_stage(
            True, x_s, lat_tiles, mod_l, lat_row, wl, consts, tables)
        for dst, val in zip(caches, (ka_f, va_f, ckv_f, kr_f)):
            dst.append(val)
        oa_c, ob_c = _ctx_attention(sink, qa_c, kd_c, va_c, qb_c, kb_c, vb_c, n_ctx_seq, ctx_len)
        oa_s = _window_attention(sink, qa_s, kd_s, va_s, kd_ctx_all[:, l], vt_swa_ctx_all[:, l], n_lat_seq,
                                 lat_len)
        ob_s = _latent_mla(qb_s, kb_s, vb_s, kb_ctx_all[l], vt_ctx_all[l], n_lat_seq, lat_len)
        x1_c, h2_c, route_c = _finish(x_c, oa_c, ob_c, oc_c, mod_l, ctx_row, wl)
        x1_s, h2_s, route_s = _finish(x_s, oa_s, ob_s, oc_s, mod_l, lat_row, wl)
        x_c = _moe(h2_c, route_c, x1_c, mod_l, ctx_row, wl, consts["tri"])
        x_s = _moe(h2_s, route_s, x1_s, mod_l, lat_moe_row, wl, consts["tri"])

    y_prompt = x_c.reshape(n_ctx_seq, ctx_len, d)
    y_sample = x_s.reshape(n_lat_seq, lat_len, d)
    stack = lambda vals, tail: jnp.stack([v.reshape((n_ctx_seq, ctx_len) + tail) for v in vals], axis=1)
    return (y_prompt, y_sample,
            stack(caches[0], (SWA_KV_HEADS, SWA_HEAD_DIM)), stack(caches[1], (SWA_KV_HEADS, SWA_HEAD_DIM)),
            stack(caches[2], (MLA_KV_LORA,)), stack(caches[3], (MLA_ROPE,)))
```

```python
import functools

import numpy as np
import jax
import jax.numpy as jnp
from jax import lax
from jax.experimental import pallas as pl
from jax.experimental.pallas import tpu as pltpu

F32 = jnp.float32
BF16 = jnp.bfloat16

D_MODEL = 1024
GRID_W = 64
ROPE_THETA = 10000.0
EPS = 1e-6
NEG_INF = -1e30

SWA_HEADS = 8
SWA_KV_HEADS = 2
SWA_GROUP = SWA_HEADS // SWA_KV_HEADS
SWA_HEAD_DIM = 64
SWA_WINDOW = 128
SWA_Q_COLS = SWA_HEADS * SWA_HEAD_DIM
SWA_KV_COLS = SWA_KV_HEADS * SWA_HEAD_DIM

MLA_HEADS = 8
MLA_Q_LORA = 256
MLA_KV_LORA = 128
MLA_NOPE = 64
MLA_ROPE = 32
MLA_V = 64
MLA_QK = MLA_NOPE + MLA_ROPE

GM_CHUNK = 128
GM_GROUPS = 4
GM_WIDTH = 512

N_EXPERT_GROUPS = 4
EXPERTS_PER_GROUP = 4
N_EXPERTS = N_EXPERT_GROUPS * EXPERTS_PER_GROUP
EXPERT_FF = 512

LANES = 128
MXU_DIM = 256
VMEM_LIMIT_BYTES = 56 * 1024 * 1024

HEAD_PAD = LANES
MLA_PAD_COLS = MLA_HEADS * HEAD_PAD
MLA_V_ROWS = HEAD_PAD
MLA_VT_ROWS = MLA_HEADS * MLA_V_ROWS

TOKEN_TILE = 512
MLA_Q_TILE = 512
MLA_KEY_CHUNK = 512
MLA_SCORE_LOOKAHEAD = 2
MLA_MIN_DENOMINATOR = 2.0 ** -90
LOG2_E = float(np.log2(np.e))
MOE_TILE_CANDIDATES = (1024, 512, 256)
MOE_CHUNK = 320
MOE_EXPERTS_PER_STEP = 2

_QA0, _KA0, _VA0, _CQ0, _CKV0, _KR0, _U0, _V0, _ZCOLS = 0, 512, 640, 768, 1024, 1152, 1280, 1792, 2304
_GATE0 = SWA_Q_COLS + 2 * SWA_KV_COLS + MLA_Q_LORA + MLA_KV_LORA + MLA_ROPE + 2 * GM_WIDTH

ROUTER_LANE0 = N_EXPERT_GROUPS


def _cparams(*sem):
    return pltpu.CompilerParams(dimension_semantics=sem, vmem_limit_bytes=VMEM_LIMIT_BYTES)


def _nt_dot(a, b):
    return lax.dot_general(a, b, (((1,), (1,)), ((), ())), preferred_element_type=F32)


def _dot(a, b):
    return jnp.dot(a, b, preferred_element_type=F32)


def _gelu_tanh(x):
    return 0.5 * x * (1.0 + jnp.tanh(np.sqrt(2.0 / np.pi) * (x + 0.044715 * (x * x * x))))


def _group_sumsq(v, ones_ref):
    width = ones_ref.shape[0]
    parts = []
    for c in range(v.shape[1] // width):
        blk = v[:, c * width:(c + 1) * width]
        parts.append(_dot((blk * blk).astype(BF16), ones_ref[...]))
    return parts[0] if len(parts) == 1 else jnp.concatenate(parts, axis=1)


def _rope(v, cos, sin, rot_ref):
    width = rot_ref.shape[0]
    parts = []
    for c in range(v.shape[1] // width):
        blk = v[:, c * width:(c + 1) * width]
        rot = _dot(blk.astype(BF16), rot_ref[...])
        for s in range(width // LANES):
            sl = slice(s * LANES, (s + 1) * LANES)
            parts.append(blk[:, sl] * cos + rot[:, sl] * sin)
    return parts[0] if len(parts) == 1 else jnp.concatenate(parts, axis=1)


def _modulated_norm(x, gain, shift, scale):
    ms = jnp.mean(x * x, axis=-1, keepdims=True)
    h = x * lax.rsqrt(ms + EPS) * gain
    return h * (1.0 + scale) + shift


def _adaln_kernel(cond_ref, w_ref, b_ref, o_ref):
    a = cond_ref[...]
    a = a * jax.nn.sigmoid(a)
    o_ref[...] = _dot(a.astype(BF16), w_ref[...].astype(BF16)) + b_ref[...]


def _adaln(cond8, w_mod, b_mod):
    depth, d, n = w_mod.shape
    tn = 1536
    return pl.pallas_call(
        _adaln_kernel,
        grid=(depth, n // tn),
        in_specs=[
            pl.BlockSpec((8, d), lambda l, j: (0, 0)),
            pl.BlockSpec((None, d, tn), lambda l, j: (l, 0, j)),
            pl.BlockSpec((None, 1, tn), lambda l, j: (l, 0, j)),
        ],
        out_specs=pl.BlockSpec((None, 8, tn), lambda l, j: (l, 0, j)),
        out_shape=jax.ShapeDtypeStruct((depth, 8, n), F32),
        compiler_params=_cparams("parallel", "parallel"),
        name="adaln",
    )(cond8, w_mod, b_mod.reshape(depth, 1, n))


def _mla_keys_values(ckv_n, kr_hi, wkn_ref, wv_ref, gk1_ref, gk2_ref, ones128_ref, rope_args):
    cb = ckv_n.astype(BF16)
    nope = _dot(cb, wkn_ref[...])
    vals = _dot(cb, wv_ref[...])
    kr_ss = jnp.sum(kr_hi * kr_hi, axis=-1, keepdims=True)
    inv = lax.rsqrt((_group_sumsq(nope, ones128_ref) + kr_ss) * (1.0 / MLA_QK) + EPS)
    krg = kr_hi * gk2_ref[...]
    if rope_args is not None:
        cos, sin, rot_ref = rope_args
        krg = _rope(krg, cos, sin, rot_ref)
    gk1 = gk1_ref[...]
    parts = []
    for h in range(MLA_HEADS):
        sl = slice(h * HEAD_PAD, (h + 1) * HEAD_PAD)
        parts.append((nope[:, sl] * gk1[:, sl] + krg) * inv[:, sl])
    return jnp.concatenate(parts, axis=1), vals


def _values_with_sum_rows(vals_pad, vone_ref):
    return (vals_pad + vone_ref[...]).T.astype(BF16)


def _ctx_keys_kernel(ckv_ref, kr_ref, wkn_ref, wv_ref, gk1_ref, gk2_ref, vone_ref, ones128_ref, k_ref, vt_ref):
    keys, vals = _mla_keys_values(ckv_ref[...], kr_ref[...], wkn_ref, wv_ref, gk1_ref, gk2_ref,
                                  ones128_ref, None)
    k_ref[...] = keys.astype(BF16)
    vt_ref[...] = _values_with_sum_rows(vals, vone_ref)


def _ctx_keys(cache_ckv, cache_kr_hi, wkn, wvp, gk1, gk2, vone, ones128):
    nb, depth, past, _ = cache_ckv.shape
    cache_map = lambda l, b: (b, l, 0, 0)
    w_map = lambda l, b: (l, 0, 0)
    return pl.pallas_call(
        _ctx_keys_kernel,
        grid=(depth, nb),
        in_specs=[
            pl.BlockSpec((None, None, past, MLA_KV_LORA), cache_map),
            pl.BlockSpec((None, None, past, LANES), cache_map),
            pl.BlockSpec((None, MLA_KV_LORA, MLA_PAD_COLS), w_map),
            pl.BlockSpec((None, MLA_KV_LORA, MLA_VT_ROWS), w_map),
            pl.BlockSpec((None, 1, MLA_PAD_COLS), w_map),
            pl.BlockSpec((None, 1, LANES), w_map),
            pl.BlockSpec((1, MLA_VT_ROWS), lambda l, b: (0, 0)),
            pl.BlockSpec((MXU_DIM, MXU_DIM), lambda l, b: (0, 0)),
        ],
        out_specs=[
            pl.BlockSpec((None, None, past, MLA_PAD_COLS), lambda l, b: (l, b, 0, 0)),
            pl.BlockSpec((None, None, MLA_VT_ROWS, past), lambda l, b: (l, b, 0, 0)),
        ],
        out_shape=[
            jax.ShapeDtypeStruct((depth, nb, past, MLA_PAD_COLS), BF16),
            jax.ShapeDtypeStruct((depth, nb, MLA_VT_ROWS, past), BF16),
        ],
        compiler_params=_cparams("parallel", "parallel"),
        name="ctx_keys",
    )(cache_ckv, cache_kr_hi, wkn, wvp, gk1, gk2, vone, ones128)


def _token_kernel(latent, x_ref, mod_ref, gmix_ref, wa_ref, gqa_ref, gka_ref, gcq_ref, wuq_ref, gq_ref,
                  gckv_ref, wkn_ref, wv_ref, gk1_ref, gk2_ref, lng_ref, lnb_ref, ws_ref, bs_ref,
                  ones64_ref, ones128_ref, *rest):
    if latent:
        (cosa_ref, sina_ref, cosb_ref, sinb_ref, rota_ref, rotb_ref, vone_ref,
         qa_o, kd_o, va_o, qb_o, kb_o, vb_o, oc_o) = rest
    else:
        (qa_o, kd_o, va_o, qb_o, kb_o, vb_o, oc_o, kac_o, vac_o, ckvc_o, krc_o) = rest
    d = D_MODEL
    rows = x_ref.shape[0]
    h = _modulated_norm(x_ref[...], gmix_ref[...], mod_ref[:, 0:d], mod_ref[:, d:2 * d])
    z = _dot(h.astype(BF16), wa_ref[...])

    qa = z[:, _QA0:_KA0]
    qa = qa * lax.rsqrt(_group_sumsq(qa, ones64_ref) * (1.0 / SWA_HEAD_DIM) + EPS) * gqa_ref[...]
    ka = z[:, _KA0:_VA0]
    ka_ss = _dot((ka * ka).astype(BF16), ones64_ref[0:LANES, 0:LANES])
    ka = ka * lax.rsqrt(ka_ss * (1.0 / SWA_HEAD_DIM) + EPS) * gka_ref[...]
    va = z[:, _VA0:_CQ0]
    if latent:
        qa = _rope(qa, cosa_ref[...], sina_ref[...], rota_ref)
        ka_r = _rope(ka, cosa_ref[...], sina_ref[...], rota_ref.at[0:LANES, 0:LANES])
    else:
        kac_o[...] = ka
        vac_o[...] = va
        ka_r = ka
    swapped = pltpu.roll(ka_r, SWA_HEAD_DIM, 1)
    low = lax.broadcasted_iota(jnp.int32, (rows, LANES), 1) < SWA_HEAD_DIM
    qa_o[...] = qa.astype(BF16)
    kd_o[:, 0:LANES] = jnp.where(low, ka_r, swapped).astype(BF16)
    kd_o[:, LANES:2 * LANES] = jnp.where(low, swapped, ka_r).astype(BF16)
    va_o[...] = va.T.astype(BF16) if latent else va.astype(BF16)

    cq = z[:, _CQ0:_CKV0]
    cq = cq * lax.rsqrt(jnp.mean(cq * cq, axis=-1, keepdims=True) + EPS) * gcq_ref[...]
    qb = _dot(cq.astype(BF16), wuq_ref[...])
    qb = qb * lax.rsqrt(_group_sumsq(qb, ones128_ref) * (1.0 / MLA_QK) + EPS) * gq_ref[...]
    if latent:
        qb = _rope(qb, cosb_ref[...], sinb_ref[...], rotb_ref)
    qb_o[...] = qb.astype(BF16)

    ckv = z[:, _CKV0:_KR0]
    ckv = ckv * lax.rsqrt(jnp.mean(ckv * ckv, axis=-1, keepdims=True) + EPS) * gckv_ref[...]
    krb = z[:, _KR0:_U0]
    lane = lax.broadcasted_iota(jnp.int32, (rows, LANES), 1)
    kr_hi = jnp.where(lane >= MLA_NOPE, krb, 0.0)
    rope_args = (cosb_ref[...], sinb_ref[...], rotb_ref.at[0:LANES, 0:LANES]) if latent else None
    kb, vb = _mla_keys_values(ckv, kr_hi, wkn_ref, wv_ref, gk1_ref, gk2_ref, ones128_ref, rope_args)
    kb_o[...] = kb.astype(BF16)
    if latent:
        vb_o[...] = _values_with_sum_rows(vb, vone_ref)
    else:
        vb_o[...] = vb.astype(BF16)
    if not latent:
        ckvc_o[...] = ckv
        krc_o[...] = krb[:, 0:MLA_ROPE]

    gu = _gelu_tanh(z[:, _U0:_V0])
    gv = _gelu_tanh(z[:, _V0:_ZCOLS])
    gc = gv - jnp.mean(gv, axis=-1, keepdims=True)
    var = jnp.mean(gc * gc, axis=-1, keepdims=True)
    vg = (gc * lax.rsqrt(var + EPS) * lng_ref[...] + lnb_ref[...]).astype(BF16)
    n_chunks = rows // GM_CHUNK
    for g in range(GM_GROUPS):
        gl = slice(g * LANES, (g + 1) * LANES)
        rhs = jnp.concatenate([vg[c * GM_CHUNK:(c + 1) * GM_CHUNK, gl] for c in range(n_chunks)], axis=1)
        mixed = _dot(ws_ref[g], rhs)
        for c in range(n_chunks):
            rs = slice(c * GM_CHUNK, (c + 1) * GM_CHUNK)
            oc_o[rs, gl] = (gu[rs, gl] * (mixed[:, c * LANES:(c + 1) * LANES] + bs_ref[:, gl])).astype(BF16)


def _token_stage(latent, x, n_tiles, mod_l, mod_row, wl, consts, tables):
    t = TOKEN_TILE
    d = D_MODEL
    n = n_tiles * t
    full = lambda a: pl.BlockSpec(a.shape, lambda i: (0,) * a.ndim)
    weights = [wl["g_mix"], wl["w_a"], wl["gqa"], wl["gka"], wl["gcq"], wl["w_uq"], wl["gq"], wl["gckv"],
               wl["w_kn"], wl["w_vp"] if latent else wl["w_v"], wl["gk1"], wl["gk2"], wl["ln_g"], wl["ln_b"], wl["w_s"], wl["b_s"],
               consts["ones64"], consts["ones128"]]
    in_specs = [
        pl.BlockSpec((t, d), lambda i: (i, 0)),
        pl.BlockSpec((None, 1, 6 * d), lambda i: (mod_row(i), 0, 0)),
    ] + [full(a) for a in weights]
    args = [x, mod_l] + weights
    row = lambda w: pl.BlockSpec((t, w), lambda i: (i, 0))
    out_widths = [SWA_Q_COLS, 2 * LANES, LANES, MLA_PAD_COLS, MLA_PAD_COLS, MLA_HEADS * MLA_V, GM_WIDTH]
    out_specs = [row(w) for w in out_widths]
    out_shape = [jax.ShapeDtypeStruct((n, w), BF16) for w in out_widths]
    if latent:
        tiles_per_seq = tables["cos_a"].shape[0] // t
        tab = lambda: pl.BlockSpec((t, LANES), lambda i: (i % tiles_per_seq, 0))
        in_specs += [tab(), tab(), tab(), tab(), full(consts["rot_a"]), full(consts["rot_b"]),
                     full(consts["v_ones"])]
        args += [tables["cos_a"], tables["sin_a"], tables["cos_b"], tables["sin_b"],
                 consts["rot_a"], consts["rot_b"], consts["v_ones"]]
        out_specs[5] = pl.BlockSpec((MLA_VT_ROWS, t), lambda i: (0, i))
        out_shape[5] = jax.ShapeDtypeStruct((MLA_VT_ROWS, n), BF16)
        out_specs[2] = pl.BlockSpec((LANES, t), lambda i: (0, i))
        out_shape[2] = jax.ShapeDtypeStruct((LANES, n), BF16)
    else:
        cache_widths = [SWA_KV_COLS, SWA_KV_COLS, MLA_KV_LORA, MLA_ROPE]
        out_specs += [row(w) for w in cache_widths]
        out_shape += [jax.ShapeDtypeStruct((n, w), F32) for w in cache_widths]
    return pl.pallas_call(
        functools.partial(_token_kernel, latent),
        grid=(n_tiles,),
        in_specs=in_specs,
        out_specs=out_specs,
        out_shape=out_shape,
        compiler_params=_cparams("parallel"),
        name="token_stage_latent" if latent else "token_stage_ctx",
    )(*args)


def _softmax_pv(scores, values, extra=None):
    m = jnp.max(scores[0], axis=-1, keepdims=True)
    for s in scores[1:]:
        m = jnp.maximum(m, jnp.max(s, axis=-1, keepdims=True))
    if extra is not None:
        m = jnp.maximum(m, extra)
    den = jnp.exp(extra - m) if extra is not None else 0.0
    acc = None
    for s, v in zip(scores, values):
        p = jnp.exp(s - m)
        den = den + jnp.sum(p, axis=-1, keepdims=True)
        pv = _dot(p.astype(BF16), v)
        acc = pv if acc is None else acc + pv
    return acc / den


def _ctx_attn_kernel(sink_ref, qa_ref, kd_ref, va_ref, qb_ref, kb_ref, vb_ref, oa_ref, ob_ref):
    rows = qa_ref.shape[0]
    low = lax.broadcasted_iota(jnp.int32, (rows, LANES), 1) < SWA_HEAD_DIM
    va = va_ref[...]
    res = []
    for hd in range(SWA_HEADS):
        qblk = qa_ref[:, (hd // 2) * LANES:(hd // 2 + 1) * LANES]
        keep = low if hd % 2 == 0 else jnp.logical_not(low)
        qm = jnp.where(keep, qblk, jnp.zeros_like(qblk))
        kvh = hd // SWA_GROUP
        s = _nt_dot(qm, kd_ref[:, kvh * LANES:(kvh + 1) * LANES]) * (SWA_HEAD_DIM ** -0.5)
        res.append(_softmax_pv([s], [va], sink_ref[hd]))
    for j in range(SWA_GROUP):
        oa_ref[:, j * LANES:(j + 1) * LANES] = jnp.where(low, res[j], res[SWA_GROUP + j]).astype(BF16)
    res = []
    for h in range(MLA_HEADS):
        sl = slice(h * HEAD_PAD, (h + 1) * HEAD_PAD)
        s = _nt_dot(qb_ref[:, sl], kb_ref[:, sl]) * (MLA_QK ** -0.5)
        res.append(_softmax_pv([s], [vb_ref[:, (h // 2) * LANES:(h // 2 + 1) * LANES]]))
    for j in range(MLA_HEADS // 2):
        ob_ref[:, j * LANES:(j + 1) * LANES] = jnp.where(low, res[2 * j], res[2 * j + 1]).astype(BF16)


def _ctx_attention(sink, qa, kd, va, qb, kb, vb, n_seq, seq):
    row = lambda w: pl.BlockSpec((seq, w), lambda b: (b, 0))
    widths = [SWA_Q_COLS, 2 * LANES, LANES, MLA_PAD_COLS, MLA_PAD_COLS, MLA_HEADS * MLA_V]
    return pl.pallas_call(
        _ctx_attn_kernel,
        grid=(n_seq,),
        in_specs=[pl.BlockSpec(memory_space=pltpu.SMEM)] + [row(w) for w in widths],
        out_specs=[row(SWA_Q_COLS), row(MLA_HEADS * MLA_V)],
        out_shape=[jax.ShapeDtypeStruct((n_seq * seq, SWA_Q_COLS), BF16),
                   jax.ShapeDtypeStruct((n_seq * seq, MLA_HEADS * MLA_V), BF16)],
        compiler_params=_cparams("parallel"),
        name="ctx_attention",
    )(sink, qa, kd, va, qb, kb, vb)


def _window_attn_kernel(sink_ref, q_ref, kp_ref, kc_ref, kn_ref, vtp_ref, vtc_ref, vtn_ref, kctx_ref,
                        vtctx_ref, o_ref):
    w = SWA_WINDOW
    i = pl.program_id(1)
    nb = pl.num_programs(1)
    cols = SWA_GROUP * w
    c = (SWA_HEAD_DIM ** -0.5) * LOG2_E
    low = lax.broadcasted_iota(jnp.int32, (w, LANES), 1) < SWA_HEAD_DIM
    m_io = lax.broadcasted_iota(jnp.int32, (3 * w, cols), 0)
    rel = m_io - (lax.broadcasted_iota(jnp.int32, (3 * w, cols), 1) & (w - 1))
    first_row = jnp.where(i == 0, w, 0)
    end_row = jnp.where(i == nb - 1, 2 * w, 3 * w)
    valid = (rel >= 0) & (rel <= 2 * w) & (m_io >= first_row) & (m_io < end_row)
    head_of_col = lax.broadcasted_iota(jnp.int32, (1, cols), 1) >> (w.bit_length() - 1)
    vt_loc = jnp.concatenate([vtp_ref[...], vtc_ref[...], vtn_ref[...]], axis=1)
    vt_ctx = vtctx_ref[...]
    top_loc = lax.broadcasted_iota(jnp.int32, vt_loc.shape, 0) < SWA_HEAD_DIM
    top_ctx = lax.broadcasted_iota(jnp.int32, vt_ctx.shape, 0) < SWA_HEAD_DIM
    staged = []
    for kvh in range(SWA_KV_HEADS):
        qs = []
        sink = jnp.zeros((1, cols), F32)
        for g in range(SWA_GROUP):
            hd = kvh * SWA_GROUP + g
            qblk = q_ref[:, (hd // 2) * LANES:(hd // 2 + 1) * LANES]
            keep = low if hd % 2 == 0 else jnp.logical_not(low)
            qs.append(jnp.where(keep, qblk, jnp.zeros_like(qblk)))
            sink = jnp.where(head_of_col == g, sink_ref[hd], sink)
        qs = jnp.concatenate(qs, axis=0)
        ks = slice(kvh * LANES, (kvh + 1) * LANES)
        k_loc = jnp.concatenate([kp_ref[:, ks], kc_ref[:, ks], kn_ref[:, ks]], axis=0)
        st_ctx = _nt_dot(kctx_ref[:, ks], qs)
        st_loc = jnp.where(valid, _nt_dot(k_loc, qs), NEG_INF)
        staged.append((st_ctx, st_loc, sink))
    res = []
    for kvh, (st_ctx, st_loc, sink) in enumerate(staged):
        top = jnp.maximum(jnp.max(st_ctx, axis=0, keepdims=True), jnp.max(st_loc, axis=0, keepdims=True))
        m2 = jnp.maximum(top * (SWA_HEAD_DIM ** -0.5), sink) * LOG2_E
        p_ctx = jnp.exp2(st_ctx * c - m2).astype(BF16)
        p_loc = jnp.exp2(st_loc * c - m2).astype(BF16)
        own_loc = top_loc if kvh == 0 else jnp.logical_not(top_loc)
        own_ctx = top_ctx if kvh == 0 else jnp.logical_not(top_ctx)
        acc = (_dot(jnp.where(own_ctx, vt_ctx, jnp.ones_like(vt_ctx)), p_ctx)
               + _dot(jnp.where(own_loc, vt_loc, jnp.ones_like(vt_loc)), p_loc))
        own = slice(kvh * SWA_HEAD_DIM, (kvh + 1) * SWA_HEAD_DIM)
        other = (1 - kvh) * SWA_HEAD_DIM
        den = acc[other:other + 1] + jnp.exp2(sink * LOG2_E - m2)
        res.append(acc[own] / den)
    for j in range(SWA_GROUP):
        cs = slice(j * w, (j + 1) * w)
        o_ref[:, j * LANES:(j + 1) * LANES] = jnp.concatenate([res[0][:, cs], res[1][:, cs]], axis=0).T.astype(BF16)


def _window_attention(sink, qa, kd, vat, kd_ctx, vt_ctx, n_seq, seq):
    w = SWA_WINDOW
    nb = seq // w
    past = kd_ctx.shape[1]
    cur = lambda b, i: (b * nb + i, 0)
    prev = lambda b, i: (b * nb + jnp.maximum(i - 1, 0), 0)
    nxt = lambda b, i: (b * nb + jnp.minimum(i + 1, nb - 1), 0)
    flip = lambda f: (lambda b, i: f(b, i)[::-1])
    return pl.pallas_call(
        _window_attn_kernel,
        grid=(n_seq, nb),
        in_specs=[
            pl.BlockSpec(memory_space=pltpu.SMEM),
            pl.BlockSpec((w, SWA_Q_COLS), cur),
            pl.BlockSpec((w, 2 * LANES), prev),
            pl.BlockSpec((w, 2 * LANES), cur),
            pl.BlockSpec((w, 2 * LANES), nxt),
            pl.BlockSpec((LANES, w), flip(prev)),
            pl.BlockSpec((LANES, w), flip(cur)),
            pl.BlockSpec((LANES, w), flip(nxt)),
            pl.BlockSpec((None, past, 2 * LANES), lambda b, i: (b, 0, 0)),
            pl.BlockSpec((None, LANES, past), lambda b, i: (b, 0, 0)),
        ],
        out_specs=pl.BlockSpec((w, SWA_Q_COLS), cur),
        out_shape=jax.ShapeDtypeStruct((n_seq * seq, SWA_Q_COLS), BF16),
        compiler_params=_cparams("parallel", "parallel"),
        name="window_attention",
    )(sink, qa, kd, kd, kd, vat, vat, vat, kd_ctx, vt_ctx)


def _latent_mla_kernel(bound_ref, q_ref, k_ref, vt_ref, kctx_ref, vtctx_ref, o_ref):
    kc = MLA_KEY_CHUNK
    c = (MLA_QK ** -0.5) * LOG2_E
    chunks = ([(kctx_ref, vtctx_ref, j) for j in range(kctx_ref.shape[0] // kc)]
              + [(k_ref, vt_ref, j) for j in range(k_ref.shape[0] // kc)])
    heads = [slice(e * HEAD_PAD, (e + 1) * HEAD_PAD) for e in range(2)]

    def scores(n):
        keys_ref, _, j = chunks[n]
        return [_nt_dot(keys_ref[j * kc:(j + 1) * kc, sl], q_ref[:, sl]) for sl in heads]

    def attend(update):
        state = [None, None]
        pending = [scores(n) for n in range(min(MLA_SCORE_LOOKAHEAD, len(chunks)))]
        for n, (_, vals_ref, j) in enumerate(chunks):
            if n + MLA_SCORE_LOOKAHEAD < len(chunks):
                pending.append(scores(n + MLA_SCORE_LOOKAHEAD))
            st = pending.pop(0)
            for e in range(2):
                vals = vals_ref[e * MLA_V_ROWS:(e + 1) * MLA_V_ROWS, j * kc:(j + 1) * kc]
                state[e] = update(state[e], st[e], vals)
        return state

    def write(accs):
        outs = [a[0:MLA_V] / a[MLA_V:MLA_V + 1] for a in accs]
        o_ref[...] = jnp.concatenate(outs, axis=0).T.astype(BF16)

    bound = bound_ref[0]

    def bounded(acc, st, vals):
        pv = _dot(vals, jnp.exp2(st * c - bound).astype(BF16))
        return pv if acc is None else acc + pv

    accs = attend(bounded)
    smallest = jnp.minimum(jnp.min(accs[0][MLA_V:MLA_V + 1]), jnp.min(accs[1][MLA_V:MLA_V + 1]))
    safe = smallest >= MLA_MIN_DENOMINATOR

    @pl.when(safe)
    def _():
        write(accs)

    @pl.when(jnp.logical_not(safe))
    def _():
        def online(state, st, vals):
            cmax = jnp.max(st, axis=0, keepdims=True)
            if state is None:
                return cmax, _dot(vals, jnp.exp2((st - cmax) * c).astype(BF16))
            m, acc = state
            m_new = jnp.maximum(m, cmax)
            p = jnp.exp2((st - m_new) * c).astype(BF16)
            return m_new, acc * jnp.exp2((m - m_new) * c) + _dot(vals, p)

        write([acc for _, acc in attend(online)])


def _latent_mla(bound, qb, kb, vb, k_ctx, v_ctx, n_seq, seq):
    tq = MLA_Q_TILE
    nq = seq // tq
    past = k_ctx.shape[1]
    pair = 2 * HEAD_PAD
    assert past % MLA_KEY_CHUNK == 0 and seq % MLA_KEY_CHUNK == 0
    return pl.pallas_call(
        _latent_mla_kernel,
        grid=(n_seq, MLA_HEADS // 2, nq),
        in_specs=[
            pl.BlockSpec(memory_space=pltpu.SMEM),
            pl.BlockSpec((tq, pair), lambda b, p, i: (b * nq + i, p)),
            pl.BlockSpec((seq, pair), lambda b, p, i: (b, p)),
            pl.BlockSpec((2 * MLA_V_ROWS, seq), lambda b, p, i: (p, b)),
            pl.BlockSpec((None, past, pair), lambda b, p, i: (b, 0, p)),
            pl.BlockSpec((None, 2 * MLA_V_ROWS, past), lambda b, p, i: (b, p, 0)),
        ],
        out_specs=pl.BlockSpec((tq, LANES), lambda b, p, i: (b * nq + i, p)),
        out_shape=jax.ShapeDtypeStruct((n_seq * seq, MLA_HEADS * MLA_V), BF16),
        compiler_params=_cparams("parallel", "parallel", "parallel"),
        name="latent_mla",
    )(bound, qb, kb, vb, k_ctx, v_ctx)


def _route(logits):
    lane = lax.broadcasted_iota(jnp.int32, logits.shape, 1).astype(F32)
    big = float(4 * LANES)
    gmask = lane < N_EXPERT_GROUPS
    gmax = jnp.max(jnp.where(gmask, logits, -jnp.inf), axis=-1, keepdims=True)
    gsum = jnp.sum(jnp.where(gmask, jnp.exp(logits - gmax), 0.0), axis=-1, keepdims=True)
    g_w = 1.0 / gsum
    g_idx = jnp.min(jnp.where(gmask & (logits == gmax), lane, big), axis=-1, keepdims=True)
    first = ROUTER_LANE0 + EXPERTS_PER_GROUP * g_idx
    emask = (lane >= first) & (lane < first + EXPERTS_PER_GROUP)
    el = jnp.where(emask, logits, -jnp.inf)
    m1 = jnp.max(el, axis=-1, keepdims=True)
    i1 = jnp.min(jnp.where(emask & (el == m1), lane, big), axis=-1, keepdims=True)
    el2 = jnp.where(lane == i1, -jnp.inf, el)
    m2 = jnp.max(el2, axis=-1, keepdims=True)
    i2 = jnp.min(jnp.where(emask & (el2 == m2), lane, big), axis=-1, keepdims=True)
    esum = jnp.sum(jnp.where(emask, jnp.exp(logits - m1), 0.0), axis=-1, keepdims=True)
    p1 = 1.0 / esum
    p2 = jnp.exp(m2 - m1) / esum
    tot = p1 + p2
    combine = jnp.where(lane == i1, g_w * (p1 / tot), 0.0) + jnp.where(lane == i2, g_w * (p2 / tot), 0.0)
    return jnp.where(lane == g_idx, 1.0, combine)


def _finish_kernel(x_ref, oa_ref, ob_ref, oc_ref, mod_ref, gmix_ref, gffn_ref, wg_ref, woa_ref, wob_ref,
                   woc_ref, wout_ref, wrh_ref, wrl_ref, br_ref, x1_ref, h2_ref, comb_ref):
    d = D_MODEL
    x = x_ref[...]
    h = _modulated_norm(x, gmix_ref[...], mod_ref[:, 0:d], mod_ref[:, d:2 * d]).astype(BF16)
    gates = _dot(h, wg_ref[...])
    merged = (jax.nn.sigmoid(gates[:, 0:d]) * _dot(oa_ref[...], woa_ref[...])
              + jax.nn.sigmoid(gates[:, d:2 * d]) * _dot(ob_ref[...], wob_ref[...])
              + jax.nn.sigmoid(gates[:, 2 * d:3 * d]) * _dot(oc_ref[...], woc_ref[...]))
    x1 = x + mod_ref[:, 2 * d:3 * d] * _dot(merged.astype(BF16), wout_ref[...])
    x1_ref[...] = x1
    h2 = _modulated_norm(x1, gffn_ref[...], mod_ref[:, 3 * d:4 * d], mod_ref[:, 4 * d:5 * d])
    h2_hi = h2.astype(BF16)
    h2_lo = (h2 - h2_hi.astype(F32)).astype(BF16)
    h2_ref[...] = h2_hi
    logits = (_dot(h2_hi, wrh_ref[...]) + _dot(h2_lo, wrh_ref[...]) + _dot(h2_hi, wrl_ref[...])) + br_ref[...]
    comb_ref[...] = _route(logits)


def _finish(x_all, oa, ob, oc, mod_l, mod_row, wl):
    t = TOKEN_TILE
    d = D_MODEL
    n = x_all.shape[0]
    full = lambda a: pl.BlockSpec(a.shape, lambda i: (0,) * a.ndim)
    row = lambda w: pl.BlockSpec((t, w), lambda i: (i, 0))
    weights = [wl["g_mix"], wl["g_ffn"], wl["w_g"], wl["w_o_a"], wl["w_o_b"], wl["w_o_c"], wl["w_out"],
               wl["w_r_hi"], wl["w_r_lo"], wl["b_r"]]
    return pl.pallas_call(
        _finish_kernel,
        grid=(n // t,),
        in_specs=[row(d), row(SWA_Q_COLS), row(MLA_HEADS * MLA_V), row(GM_WIDTH),
                  pl.BlockSpec((None, 1, 6 * d), lambda i: (mod_row(i), 0, 0))] + [full(a) for a in weights],
        out_specs=[row(d), row(d), row(LANES)],
        out_shape=[jax.ShapeDtypeStruct((n, d), F32), jax.ShapeDtypeStruct((n, d), BF16),
                   jax.ShapeDtypeStruct((n, LANES), F32)],
        compiler_params=_cparams("parallel"),
        name="finish",
    )(x_all, oa, ob, oc, mod_l, *weights)


def _moe_kernel(h_ref, route_ref, x1_ref, mod_ref, tri_ref, wg_ref, wu_ref, wd_ref, o_ref,
                route_t, rank_rows, rank_cols, route_split, xg, yacc, wsel):
    d = D_MODEL
    t = h_ref.shape[0]
    ch = MOE_CHUNK
    s = pl.program_id(1)
    steps_per_group = EXPERTS_PER_GROUP // MOE_EXPERTS_PER_STEP
    g = s // steps_per_group
    j0 = (s % steps_per_group) * MOE_EXPERTS_PER_STEP

    @pl.when(s == 0)
    def _():
        rt = route_ref[...].T
        route_t[...] = rt
        rr = _dot(rt.astype(BF16), tri_ref[...])
        rank_rows[...] = rr
        rank_cols[...] = rr.T
        route = route_ref[...]
        hi = route.astype(BF16)
        route_split[:, 0:LANES] = hi
        route_split[:, LANES:2 * LANES] = (route - hi.astype(F32)).astype(BF16)
        o_ref[...] = jnp.zeros_like(o_ref)

    ind_row = route_t[pl.ds(g, 1), :]
    n_rows = jnp.sum(ind_row).astype(jnp.int32)
    n_chunks = (n_rows + (ch - 1)) // ch

    def chunk_rows(c):
        return pl.ds(pl.multiple_of(c * ch, ch), ch)

    @pl.when(j0 == 0)
    def _():
        rank_row = rank_rows[pl.ds(g, 1), :]
        member = ind_row > 0.5
        r_io = lax.broadcasted_iota(jnp.int32, (ch, t), 0).astype(F32)

        def gather(c, carry):
            sel = (rank_row == r_io + (c * ch).astype(F32)) & member
            sel = jnp.where(sel, 1.0, 0.0).astype(BF16)
            rows = chunk_rows(c)
            xg[rows, :] = _dot(sel, h_ref[...]).astype(BF16)
            w2 = _dot(sel, route_split[...])
            wsel[rows, :] = w2[:, 0:LANES] + w2[:, LANES:2 * LANES]
            yacc[rows, :] = jnp.zeros((ch, d), F32)
            return carry

        lax.fori_loop(0, n_chunks, gather, 0)

    def ffn(c, carry):
        rows = chunk_rows(c)
        x = xg[rows, :]
        lane = lax.broadcasted_iota(jnp.int32, (ch, LANES), 1)
        wt = wsel[rows, :]
        total = None
        for k in range(MOE_EXPERTS_PER_STEP):
            a = _dot(x, wg_ref[k])
            hid = (a * jax.nn.sigmoid(a)) * _dot(x, wu_ref[k])
            y = _dot(hid.astype(BF16), wd_ref[k])
            w_col = jnp.sum(jnp.where(lane == ROUTER_LANE0 + EXPERTS_PER_GROUP * g + j0 + k, wt, 0.0), axis=-1,
                            keepdims=True)
            total = w_col * y if total is None else total + w_col * y
        yacc[rows, :] += total
        return carry

    lax.fori_loop(0, n_chunks, ffn, 0)

    @pl.when(j0 == EXPERTS_PER_GROUP - MOE_EXPERTS_PER_STEP)
    def _():
        lane_t = lax.broadcasted_iota(jnp.int32, (t, LANES), 1)
        member = jnp.sum(jnp.where(lane_t == g, route_ref[...], 0.0), axis=-1, keepdims=True) > 0.5
        rank_col = jnp.sum(jnp.where(lane_t == g, rank_cols[...], 0.0), axis=-1, keepdims=True)
        c_io = lax.broadcasted_iota(jnp.int32, (t, ch), 1).astype(F32)

        def scatter(c, carry):
            sel_t = (rank_col == c_io + (c * ch).astype(F32)) & member
            o_ref[...] += _dot(jnp.where(sel_t, 1.0, 0.0).astype(BF16), yacc[chunk_rows(c), :].astype(BF16))
            return carry

        lax.fori_loop(0, n_chunks, scatter, 0)

    @pl.when(s == pl.num_programs(1) - 1)
    def _():
        o_ref[...] = x1_ref[...] + mod_ref[:, 5 * d:6 * d] * o_ref[...]


def _moe(h2, route, x1, mod_l, mod_row, wl, tri):
    d = D_MODEL
    n = h2.shape[0]
    tile = tri.shape[0]
    chunk_rows = pl.cdiv(tile, MOE_CHUNK) * MOE_CHUNK
    eps = MOE_EXPERTS_PER_STEP
    row = lambda w: pl.BlockSpec((tile, w), lambda i, e: (i, 0))
    return pl.pallas_call(
        _moe_kernel,
        grid=(n // tile, N_EXPERTS // eps),
        in_specs=[row(d), row(LANES), row(d),
                  pl.BlockSpec((None, 1, 6 * d), lambda i, e: (mod_row(i), 0, 0)),
                  pl.BlockSpec((tile, tile), lambda i, e: (0, 0)),
                  pl.BlockSpec((eps, d, EXPERT_FF), lambda i, e: (e, 0, 0)),
                  pl.BlockSpec((eps, d, EXPERT_FF), lambda i, e: (e, 0, 0)),
                  pl.BlockSpec((eps, EXPERT_FF, d), lambda i, e: (e, 0, 0))],
        out_specs=row(d),
        out_shape=jax.ShapeDtypeStruct((n, d), F32),
        scratch_shapes=[pltpu.VMEM((LANES, tile), F32), pltpu.VMEM((LANES, tile), F32),
                        pltpu.VMEM((tile, LANES), F32), pltpu.VMEM((tile, 2 * LANES), BF16),
                        pltpu.VMEM((chunk_rows, d), BF16),
                        pltpu.VMEM((chunk_rows, d), F32), pltpu.VMEM((chunk_rows, LANES), F32)],
        compiler_params=_cparams("parallel", "arbitrary"),
        name="moe_grouped",
    )(h2, route, x1, mod_l, tri, wl["w_gate"], wl["w_up"], wl["w_down"])


def _block_ones(group):
    idx = np.arange(MXU_DIM)
    return jnp.asarray((idx[:, None] // group) == (idx[None, :] // group), BF16)


def _rotate_half_matrix(block, rot_start, rot_dim):
    half = rot_dim // 4
    r = np.zeros((MXU_DIM, MXU_DIM), np.float32)
    for j in range(MXU_DIM):
        o = j % block - rot_start
        if 0 <= o < rot_dim:
            partner = j + half if (o % (2 * half)) < half else j - half
            r[partner, j] = 1.0
    return jnp.asarray(r, BF16)


def _rope_tables(n_tokens, block, rot_start, rot_dim):
    half = rot_dim // 4
    tok = np.arange(n_tokens)
    pos = np.stack([tok // GRID_W, tok % GRID_W], axis=1).astype(np.float32)
    inv_freq = (ROPE_THETA ** (-np.arange(half, dtype=np.float32) / half)).astype(np.float32)
    lane = np.arange(LANES)
    o = lane % block - rot_start
    rot = (o >= 0) & (o < rot_dim)
    oc = np.where(rot, o, 0)
    axis = oc // (2 * half)
    freq = inv_freq[oc % half]
    sign = np.where((oc % (2 * half)) < half, -1.0, 1.0).astype(np.float32)
    ang = jnp.asarray(pos[:, axis]) * jnp.asarray(freq)[None, :]
    rot_j = jnp.asarray(rot)[None, :]
    cos = jnp.where(rot_j, jnp.cos(ang), 1.0)
    sin = jnp.where(rot_j, jnp.sin(ang) * jnp.asarray(sign)[None, :], 0.0)
    return cos.astype(F32), sin.astype(F32)


def _pad_heads(w, heads, width, padded=HEAD_PAD):
    lead = w.shape[:-1]
    w = w.reshape(lead + (heads, width))
    w = jnp.pad(w, [(0, 0)] * len(lead) + [(0, 0), (0, padded - width)])
    return w.reshape(lead + (heads * padded,))


def _prep_weights(p):
    depth = p["w_in"].shape[0]
    w_in = p["w_in"]
    c = np.cumsum((0, SWA_Q_COLS, SWA_KV_COLS, SWA_KV_COLS, MLA_Q_LORA, MLA_KV_LORA, MLA_ROPE, GM_WIDTH,
                   GM_WIDTH))
    kr = w_in[..., c[5]:c[6]]
    zero = jnp.zeros_like(kr)
    w_a = jnp.concatenate([w_in[..., c[0]:c[5]], kr, zero, kr, zero, w_in[..., c[6]:c[8]]], axis=-1)
    ukv = p["mla_w_ukv"].reshape(depth, MLA_KV_LORA, MLA_HEADS, MLA_NOPE + MLA_V)
    order = np.array([0, 4, 1, 5, 2, 6, 3, 7])
    w_o_a = p["w_o_a"].reshape(depth, SWA_HEADS, SWA_HEAD_DIM, D_MODEL)[:, order].reshape(depth, SWA_Q_COLS,
                                                                                          D_MODEL)
    w_r = jnp.concatenate([p["w_rg"], p["w_re"]], axis=-1)
    w_r = jnp.pad(w_r, ((0, 0), (0, 0), (0, LANES - w_r.shape[-1])))
    w_r_hi = w_r.astype(BF16)
    b_r = jnp.concatenate([p["b_rg"], p["b_re"]], axis=-1)
    b_r = jnp.pad(b_r, ((0, 0), (0, LANES - b_r.shape[-1])))
    k_gain = p["mla_k_norm"]
    row = lambda a: a.reshape(depth, 1, a.shape[-1]).astype(F32)
    bs_full = jnp.repeat(jnp.swapaxes(p["gm_b_s"], 1, 2), LANES, axis=-1)
    return {
        "g_mix": row(p["g_mix"]), "g_ffn": row(p["g_ffn"]),
        "w_a": w_a.astype(BF16), "w_g": w_in[..., _GATE0:].astype(BF16),
        "gqa": row(jnp.tile(p["swa_q_norm"], (1, SWA_HEADS))),
        "gka": row(jnp.tile(p["swa_k_norm"], (1, SWA_KV_HEADS))),
        "gcq": row(p["mla_cq_norm"]), "gckv": row(p["mla_ckv_norm"]),
        "w_uq": _pad_heads(p["mla_w_uq"], MLA_HEADS, MLA_QK).astype(BF16),
        "gq": row(_pad_heads(jnp.tile(p["mla_q_norm"], (1, MLA_HEADS)), MLA_HEADS, MLA_QK)),
        "w_kn": _pad_heads(ukv[..., :MLA_NOPE].reshape(depth, MLA_KV_LORA, -1), MLA_HEADS, MLA_NOPE).astype(BF16),
        "w_v": ukv[..., MLA_NOPE:].reshape(depth, MLA_KV_LORA, -1).astype(BF16),
        "w_vp": _pad_heads(ukv[..., MLA_NOPE:].reshape(depth, MLA_KV_LORA, -1), MLA_HEADS, MLA_V,
                           MLA_V_ROWS).astype(BF16),
        "gk1": row(_pad_heads(jnp.tile(k_gain[:, :MLA_NOPE], (1, MLA_HEADS)), MLA_HEADS, MLA_NOPE)),
        "gk2": row(jnp.pad(k_gain[:, MLA_NOPE:], ((0, 0), (MLA_NOPE, LANES - MLA_QK)))),
        "ln_g": row(p["gm_ln_g"]), "ln_b": row(p["gm_ln_b"]),
        "w_s": p["gm_w_s"].astype(BF16), "b_s": bs_full.astype(F32),
        "w_o_a": w_o_a.astype(BF16), "w_o_b": p["w_o_b"].astype(BF16), "w_o_c": p["w_o_c"].astype(BF16),
        "w_out": p["w_out"].astype(BF16),
        "w_r_hi": w_r_hi, "w_r_lo": (w_r - w_r_hi.astype(F32)).astype(BF16), "b_r": row(b_r),
        "w_gate": p["w_gate"].astype(BF16), "w_up": p["w_up"].astype(BF16), "w_down": p["w_down"].astype(BF16),
    }


def kernel(x_prompt, x_sample, cache_swa_k, cache_swa_v, cache_mla_ckv, cache_mla_krope, c, c_ctx, w_mod, b_mod, g_mix, g_ffn, w_in, swa_q_norm, swa_k_norm, swa_sink, mla_cq_norm, mla_ckv_norm, mla_w_uq, mla_w_ukv, mla_q_norm, mla_k_norm, gm_ln_g, gm_ln_b, gm_w_s, gm_b_s, w_o_a, w_o_b, w_o_c, w_out, w_rg, b_rg, w_re, b_re, w_gate, w_up, w_down):
    d = D_MODEL
    n_ctx_seq, ctx_len, _ = x_prompt.shape
    n_lat_seq, lat_len, _ = x_sample.shape
    depth = w_in.shape[0]
    past = cache_swa_k.shape[2]
    n_ctx = n_ctx_seq * ctx_len
    n_lat = n_lat_seq * lat_len
    t = TOKEN_TILE
    assert n_ctx % t == 0 and ctx_len % GM_CHUNK == 0 and lat_len % t == 0 and lat_len % MLA_Q_TILE == 0 and lat_len % GRID_W == 0 and n_lat_seq < 8
    ctx_tiles = n_ctx // t
    lat_tiles = n_lat // t
    moe_tile = next(m for m in MOE_TILE_CANDIDATES if n_ctx % m == 0 and lat_len % m == 0)

    params = dict(w_in=w_in, g_mix=g_mix, g_ffn=g_ffn, swa_q_norm=swa_q_norm, swa_k_norm=swa_k_norm,
                  mla_cq_norm=mla_cq_norm, mla_ckv_norm=mla_ckv_norm, mla_w_uq=mla_w_uq, mla_w_ukv=mla_w_ukv,
                  mla_q_norm=mla_q_norm, mla_k_norm=mla_k_norm, gm_ln_g=gm_ln_g, gm_ln_b=gm_ln_b,
                  gm_w_s=gm_w_s, gm_b_s=gm_b_s, w_o_a=w_o_a, w_o_b=w_o_b, w_o_c=w_o_c, w_out=w_out,
                  w_rg=w_rg, b_rg=b_rg, w_re=w_re, b_re=b_re, w_gate=w_gate, w_up=w_up, w_down=w_down)
    w_all = _prep_weights(params)
    consts = {
        "ones64": _block_ones(SWA_HEAD_DIM), "ones128": _block_ones(HEAD_PAD),
        "rot_a": _rotate_half_matrix(SWA_HEAD_DIM, 0, SWA_HEAD_DIM),
        "rot_b": _rotate_half_matrix(HEAD_PAD, MLA_NOPE, MLA_ROPE),
        "v_ones": jnp.asarray((np.arange(MLA_VT_ROWS) % MLA_V_ROWS >= MLA_V).astype(np.float32)[None, :]),
        "tri": jnp.asarray(np.triu(np.ones((moe_tile, moe_tile), np.float32), k=1), BF16),
    }
    cos_a, sin_a = _rope_tables(lat_len, SWA_HEAD_DIM, 0, SWA_HEAD_DIM)
    cos_b, sin_b = _rope_tables(lat_len, HEAD_PAD, MLA_NOPE, MLA_ROPE)
    tables = {"cos_a": cos_a, "sin_a": sin_a, "cos_b": cos_b, "sin_b": sin_b}

    cond8 = jnp.zeros((8, d), F32).at[:n_lat_seq].set(c).at[n_lat_seq].set(c_ctx)
    mods = _adaln(cond8, w_mod, b_mod).reshape(depth, 8, 1, 6 * d)

    k0 = cache_swa_k[:, :, :, 0, :]
    k1 = cache_swa_k[:, :, :, 1, :]
    kd_ctx_all = jnp.concatenate([k0, k0, k1, k1], axis=-1).astype(BF16)
    vt_swa_ctx_all = jnp.swapaxes(cache_swa_v.reshape(n_lat_seq, depth, past, SWA_KV_COLS), 2, 3).astype(BF16)
    kr_hi = jnp.pad(cache_mla_krope, ((0, 0), (0, 0), (0, 0), (MLA_NOPE, LANES - MLA_QK)))
    kb_ctx_all, vt_ctx_all = _ctx_keys(cache_mla_ckv, kr_hi, w_all["w_kn"], w_all["w_vp"], w_all["gk1"],
                                       w_all["gk2"], consts["v_ones"], consts["ones128"])

    ctx_row = lambda i: n_lat_seq
    lat_row = lambda i: i // (lat_len // t)
    lat_moe_row = lambda i: i // (lat_len // moe_tile)

    x_c = x_prompt.reshape(n_ctx, d)
    x_s = x_sample.reshape(n_lat, d)
    caches = [[], [], [], []]
    for l in range(depth):
        wl = {k: v[l] for k, v in w_all.items()}
        mod_l = mods[l]
        sink = swa_sink[l].reshape(SWA_HEADS)
        (qa_c, kd_c, va_c, qb_c, kb_c, vb_c, oc_c, ka_f, va_f, ckv_f, kr_f) = _token_stage(
            False, x_c, ctx_tiles, mod_l, ctx_row, wl, consts, None)
        (qa_s, kd_s, va_s, qb_s, kb_s, vb_s, oc_s) = _token_stage(
            True, x_s, lat_tiles, mod_l, lat_row, wl, consts, tables)
        for dst, val in zip(caches, (ka_f, va_f, ckv_f, kr_f)):
            dst.append(val)
        oa_c, ob_c = _ctx_attention(sink, qa_c, kd_c, va_c, qb_c, kb_c, vb_c, n_ctx_seq, ctx_len)
        oa_s = _window_attention(sink, qa_s, kd_s, va_s, kd_ctx_all[:, l], vt_swa_ctx_all[:, l], n_lat_seq,
                                 lat_len)
        mla_bound = (MLA_QK * jnp.max(jnp.abs(mla_q_norm[l])) * jnp.max(jnp.abs(mla_k_norm[l]))
                     * (MLA_QK ** -0.5) * LOG2_E).reshape(1).astype(F32)
        ob_s = _latent_mla(mla_bound, qb_s, kb_s, vb_s, kb_ctx_all[l], vt_ctx_all[l], n_lat_seq, lat_len)
        x1_c, h2_c, route_c = _finish(x_c, oa_c, ob_c, oc_c, mod_l, ctx_row, wl)
        x1_s, h2_s, route_s = _finish(x_s, oa_s, ob_s, oc_s, mod_l, lat_row, wl)
        x_c = _moe(h2_c, route_c, x1_c, mod_l, ctx_row, wl, consts["tri"])
        x_s = _moe(h2_s, route_s, x1_s, mod_l, lat_moe_row, wl, consts["tri"])

    y_prompt = x_c.reshape(n_ctx_seq, ctx_len, d)
    y_sample = x_s.reshape(n_lat_seq, lat_len, d)
    stack = lambda vals, tail: jnp.stack([v.reshape((n_ctx_seq, ctx_len) + tail) for v in vals], axis=1)
    return (y_prompt, y_sample,
            stack(caches[0], (SWA_KV_HEADS, SWA_HEAD_DIM)), stack(caches[1], (SWA_KV_HEADS, SWA_HEAD_DIM)),
            stack(caches[2], (MLA_KV_LORA,)), stack(caches[3], (MLA_ROPE,)))
```

```python
import functools

import numpy as np
import jax
import jax.numpy as jnp
from jax import lax
from jax.experimental import pallas as pl
from jax.experimental.pallas import tpu as pltpu

F32 = jnp.float32
BF16 = jnp.bfloat16

D_MODEL = 1024
GRID_W = 64
ROPE_THETA = 10000.0
EPS = 1e-6
NEG_INF = -1e30

SWA_HEADS = 8
SWA_KV_HEADS = 2
SWA_GROUP = SWA_HEADS // SWA_KV_HEADS
SWA_HEAD_DIM = 64
SWA_WINDOW = 128
SWA_Q_COLS = SWA_HEADS * SWA_HEAD_DIM
SWA_KV_COLS = SWA_KV_HEADS * SWA_HEAD_DIM

MLA_HEADS = 8
MLA_Q_LORA = 256
MLA_KV_LORA = 128
MLA_NOPE = 64
MLA_ROPE = 32
MLA_V = 64
MLA_QK = MLA_NOPE + MLA_ROPE

GM_CHUNK = 128
GM_GROUPS = 4
GM_WIDTH = 512

N_EXPERT_GROUPS = 4
EXPERTS_PER_GROUP = 4
N_EXPERTS = N_EXPERT_GROUPS * EXPERTS_PER_GROUP
EXPERT_FF = 512

LANES = 128
MXU_DIM = 256
VMEM_LIMIT_BYTES = 56 * 1024 * 1024

HEAD_PAD = LANES
MLA_PAD_COLS = MLA_HEADS * HEAD_PAD
BF16_SUBLANES = 16
MLA_V_ROWS = HEAD_PAD
MLA_VT_ROWS = MLA_HEADS * MLA_V_ROWS

TOKEN_TILE = 512
MLA_Q_TILE = 512
MLA_KEY_CHUNK = 512
MLA_SCORE_LOOKAHEAD = 2
MLA_MIN_DENOMINATOR = 2.0 ** -90
LOG2_E = float(np.log2(np.e))
MOE_TILE_CANDIDATES = (1024, 512, 256)
MOE_CHUNK = 320
MOE_EXPERTS_PER_STEP = 2

_QA0, _KA0, _VA0, _CQ0, _CKV0, _KR0, _U0, _V0, _ZCOLS = 0, 512, 640, 768, 1024, 1152, 1280, 1792, 2304
_GATE0 = SWA_Q_COLS + 2 * SWA_KV_COLS + MLA_Q_LORA + MLA_KV_LORA + MLA_ROPE + 2 * GM_WIDTH

ROUTER_LANE0 = N_EXPERT_GROUPS


def _cparams(*sem):
    return pltpu.CompilerParams(dimension_semantics=sem, vmem_limit_bytes=VMEM_LIMIT_BYTES)


def _nt_dot(a, b):
    return lax.dot_general(a, b, (((1,), (1,)), ((), ())), preferred_element_type=F32)


def _dot(a, b):
    return jnp.dot(a, b, preferred_element_type=F32)


def _gelu_tanh(x):
    return 0.5 * x * (1.0 + jnp.tanh(np.sqrt(2.0 / np.pi) * (x + 0.044715 * (x * x * x))))


def _group_sumsq(v, ones_ref):
    width = ones_ref.shape[0]
    parts = []
    for c in range(v.shape[1] // width):
        blk = v[:, c * width:(c + 1) * width]
        parts.append(_dot((blk * blk).astype(BF16), ones_ref[...]))
    return parts[0] if len(parts) == 1 else jnp.concatenate(parts, axis=1)


def _rope(v, cos, sin, rot_ref):
    width = rot_ref.shape[0]
    parts = []
    for c in range(v.shape[1] // width):
        blk = v[:, c * width:(c + 1) * width]
        rot = _dot(blk.astype(BF16), rot_ref[...])
        for s in range(width // LANES):
            sl = slice(s * LANES, (s + 1) * LANES)
            parts.append(blk[:, sl] * cos + rot[:, sl] * sin)
    return parts[0] if len(parts) == 1 else jnp.concatenate(parts, axis=1)


def _modulated_norm(x, gain, shift, scale):
    ms = jnp.mean(x * x, axis=-1, keepdims=True)
    h = x * lax.rsqrt(ms + EPS) * gain
    return h * (1.0 + scale) + shift


def _adaln_kernel(cond_ref, w_ref, b_ref, o_ref):
    a = cond_ref[...]
    a = a * jax.nn.sigmoid(a)
    o_ref[...] = _dot(a.astype(BF16), w_ref[...].astype(BF16)) + b_ref[...]


def _adaln(cond8, w_mod, b_mod):
    depth, d, n = w_mod.shape
    tn = 1536
    return pl.pallas_call(
        _adaln_kernel,
        grid=(depth, n // tn),
        in_specs=[
            pl.BlockSpec((8, d), lambda l, j: (0, 0)),
            pl.BlockSpec((None, d, tn), lambda l, j: (l, 0, j)),
            pl.BlockSpec((None, 1, tn), lambda l, j: (l, 0, j)),
        ],
        out_specs=pl.BlockSpec((None, 8, tn), lambda l, j: (l, 0, j)),
        out_shape=jax.ShapeDtypeStruct((depth, 8, n), F32),
        compiler_params=_cparams("parallel", "parallel"),
        name="adaln",
    )(cond8, w_mod, b_mod.reshape(depth, 1, n))


def _mla_keys_values(ckv_n, kr_hi, wkn_ref, wv_ref, gk1_ref, gk2_ref, ones128_ref, rope_args):
    cb = ckv_n.astype(BF16)
    nope = _dot(cb, wkn_ref[...])
    vals = _dot(cb, wv_ref[...])
    kr_ss = jnp.sum(kr_hi * kr_hi, axis=-1, keepdims=True)
    inv = lax.rsqrt((_group_sumsq(nope, ones128_ref) + kr_ss) * (1.0 / MLA_QK) + EPS)
    krg = kr_hi * gk2_ref[...]
    if rope_args is not None:
        cos, sin, rot_ref = rope_args
        krg = _rope(krg, cos, sin, rot_ref)
    gk1 = gk1_ref[...]
    parts = []
    for h in range(MLA_HEADS):
        sl = slice(h * HEAD_PAD, (h + 1) * HEAD_PAD)
        parts.append((nope[:, sl] * gk1[:, sl] + krg) * inv[:, sl])
    return jnp.concatenate(parts, axis=1), vals


def _values_with_sum_rows(vals_pad, vone_ref):
    return (vals_pad + vone_ref[...]).T.astype(BF16)


def _ctx_keys_kernel(ckv_ref, kr_ref, wkn_ref, wv_ref, gk1_ref, gk2_ref, vone_ref, ones128_ref, k_ref, vt_ref):
    keys, vals = _mla_keys_values(ckv_ref[...], kr_ref[...], wkn_ref, wv_ref, gk1_ref, gk2_ref,
                                  ones128_ref, None)
    k_ref[...] = keys.astype(BF16)
    vt_ref[...] = _values_with_sum_rows(vals, vone_ref)


def _ctx_keys(cache_ckv, cache_kr_hi, wkn, wvp, gk1, gk2, vone, ones128):
    nb, depth, past, _ = cache_ckv.shape
    cache_map = lambda l, b: (b, l, 0, 0)
    w_map = lambda l, b: (l, 0, 0)
    return pl.pallas_call(
        _ctx_keys_kernel,
        grid=(depth, nb),
        in_specs=[
            pl.BlockSpec((None, None, past, MLA_KV_LORA), cache_map),
            pl.BlockSpec((None, None, past, LANES), cache_map),
            pl.BlockSpec((None, MLA_KV_LORA, MLA_PAD_COLS), w_map),
            pl.BlockSpec((None, MLA_KV_LORA, MLA_VT_ROWS), w_map),
            pl.BlockSpec((None, 1, MLA_PAD_COLS), w_map),
            pl.BlockSpec((None, 1, LANES), w_map),
            pl.BlockSpec((1, MLA_VT_ROWS), lambda l, b: (0, 0)),
            pl.BlockSpec((MXU_DIM, MXU_DIM), lambda l, b: (0, 0)),
        ],
        out_specs=[
            pl.BlockSpec((None, None, past, MLA_PAD_COLS), lambda l, b: (l, b, 0, 0)),
            pl.BlockSpec((None, None, MLA_VT_ROWS, past), lambda l, b: (l, b, 0, 0)),
        ],
        out_shape=[
            jax.ShapeDtypeStruct((depth, nb, past, MLA_PAD_COLS), BF16),
            jax.ShapeDtypeStruct((depth, nb, MLA_VT_ROWS, past), BF16),
        ],
        compiler_params=_cparams("parallel", "parallel"),
        name="ctx_keys",
    )(cache_ckv, cache_kr_hi, wkn, wvp, gk1, gk2, vone, ones128)


def _token_kernel(latent, x_ref, mod_ref, gmix_ref, wa_ref, gqa_ref, gka_ref, gcq_ref, wuq_ref, gq_ref,
                  gckv_ref, wkn_ref, wv_ref, gk1_ref, gk2_ref, lng_ref, lnb_ref, ws_ref, bs_ref,
                  ones64_ref, ones128_ref, *rest):
    if latent:
        (cosa_ref, sina_ref, cosb_ref, sinb_ref, rota_ref, rotb_ref, vone_ref,
         qa_o, kd_o, va_o, qb_o, kb_o, vb_o, oc_o) = rest
    else:
        (qa_o, kd_o, va_o, qb_o, kb_o, vb_o, oc_o, kac_o, vac_o, ckvc_o, krc_o) = rest
    d = D_MODEL
    rows = x_ref.shape[0]
    h = _modulated_norm(x_ref[...], gmix_ref[...], mod_ref[:, 0:d], mod_ref[:, d:2 * d])
    z = _dot(h.astype(BF16), wa_ref[...])

    qa = z[:, _QA0:_KA0]
    qa = qa * lax.rsqrt(_group_sumsq(qa, ones64_ref) * (1.0 / SWA_HEAD_DIM) + EPS) * gqa_ref[...]
    ka = z[:, _KA0:_VA0]
    ka_ss = _dot((ka * ka).astype(BF16), ones64_ref[0:LANES, 0:LANES])
    ka = ka * lax.rsqrt(ka_ss * (1.0 / SWA_HEAD_DIM) + EPS) * gka_ref[...]
    va = z[:, _VA0:_CQ0]
    if latent:
        qa = _rope(qa, cosa_ref[...], sina_ref[...], rota_ref)
        ka_r = _rope(ka, cosa_ref[...], sina_ref[...], rota_ref.at[0:LANES, 0:LANES])
    else:
        kac_o[...] = ka
        vac_o[...] = va
        ka_r = ka
    swapped = pltpu.roll(ka_r, SWA_HEAD_DIM, 1)
    low = lax.broadcasted_iota(jnp.int32, (rows, LANES), 1) < SWA_HEAD_DIM
    qa_o[...] = qa.astype(BF16)
    kd_o[:, 0:LANES] = jnp.where(low, ka_r, swapped).astype(BF16)
    kd_o[:, LANES:2 * LANES] = jnp.where(low, swapped, ka_r).astype(BF16)
    va_o[...] = va.T.astype(BF16) if latent else va.astype(BF16)

    cq = z[:, _CQ0:_CKV0]
    cq = cq * lax.rsqrt(jnp.mean(cq * cq, axis=-1, keepdims=True) + EPS) * gcq_ref[...]
    qb = _dot(cq.astype(BF16), wuq_ref[...])
    qb = qb * lax.rsqrt(_group_sumsq(qb, ones128_ref) * (1.0 / MLA_QK) + EPS) * gq_ref[...]
    if latent:
        qb = _rope(qb, cosb_ref[...], sinb_ref[...], rotb_ref)
    qb_o[...] = qb.astype(BF16)

    ckv = z[:, _CKV0:_KR0]
    ckv = ckv * lax.rsqrt(jnp.mean(ckv * ckv, axis=-1, keepdims=True) + EPS) * gckv_ref[...]
    krb = z[:, _KR0:_U0]
    lane = lax.broadcasted_iota(jnp.int32, (rows, LANES), 1)
    kr_hi = jnp.where(lane >= MLA_NOPE, krb, 0.0)
    rope_args = (cosb_ref[...], sinb_ref[...], rotb_ref.at[0:LANES, 0:LANES]) if latent else None
    kb, vb = _mla_keys_values(ckv, kr_hi, wkn_ref, wv_ref, gk1_ref, gk2_ref, ones128_ref, rope_args)
    kb_o[...] = kb.astype(BF16)
    if latent:
        vb_o[...] = _values_with_sum_rows(vb, vone_ref)
    else:
        vb_o[...] = vb.astype(BF16)
    if not latent:
        ckvc_o[...] = ckv
        krc_o[...] = krb[:, 0:MLA_ROPE]

    gu = _gelu_tanh(z[:, _U0:_V0])
    gv = _gelu_tanh(z[:, _V0:_ZCOLS])
    gc = gv - jnp.mean(gv, axis=-1, keepdims=True)
    var = jnp.mean(gc * gc, axis=-1, keepdims=True)
    vg = (gc * lax.rsqrt(var + EPS) * lng_ref[...] + lnb_ref[...]).astype(BF16)
    n_chunks = rows // GM_CHUNK
    for g in range(GM_GROUPS):
        gl = slice(g * LANES, (g + 1) * LANES)
        rhs = jnp.concatenate([vg[c * GM_CHUNK:(c + 1) * GM_CHUNK, gl] for c in range(n_chunks)], axis=1)
        mixed = _dot(ws_ref[g], rhs)
        for c in range(n_chunks):
            rs = slice(c * GM_CHUNK, (c + 1) * GM_CHUNK)
            oc_o[rs, gl] = (gu[rs, gl] * (mixed[:, c * LANES:(c + 1) * LANES] + bs_ref[:, gl])).astype(BF16)


def _token_stage(latent, x, n_tiles, mod_l, mod_row, wl, consts, tables):
    t = TOKEN_TILE
    d = D_MODEL
    n = n_tiles * t
    full = lambda a: pl.BlockSpec(a.shape, lambda i: (0,) * a.ndim)
    weights = [wl["g_mix"], wl["w_a"], wl["gqa"], wl["gka"], wl["gcq"], wl["w_uq"], wl["gq"], wl["gckv"],
               wl["w_kn"], wl["w_vp"] if latent else wl["w_v"], wl["gk1"], wl["gk2"], wl["ln_g"], wl["ln_b"], wl["w_s"], wl["b_s"],
               consts["ones64"], consts["ones128"]]
    in_specs = [
        pl.BlockSpec((t, d), lambda i: (i, 0)),
        pl.BlockSpec((None, 1, 6 * d), lambda i: (mod_row(i), 0, 0)),
    ] + [full(a) for a in weights]
    args = [x, mod_l] + weights
    row = lambda w: pl.BlockSpec((t, w), lambda i: (i, 0))
    out_widths = [SWA_Q_COLS, 2 * LANES, LANES, MLA_PAD_COLS, MLA_PAD_COLS, MLA_HEADS * MLA_V, GM_WIDTH]
    out_specs = [row(w) for w in out_widths]
    out_shape = [jax.ShapeDtypeStruct((n, w), BF16) for w in out_widths]
    if latent:
        tiles_per_seq = tables["cos_a"].shape[0] // t
        tab = lambda: pl.BlockSpec((t, LANES), lambda i: (i % tiles_per_seq, 0))
        in_specs += [tab(), tab(), tab(), tab(), full(consts["rot_a"]), full(consts["rot_b"]),
                     full(consts["v_ones"])]
        args += [tables["cos_a"], tables["sin_a"], tables["cos_b"], tables["sin_b"],
                 consts["rot_a"], consts["rot_b"], consts["v_ones"]]
        out_specs[5] = pl.BlockSpec((MLA_VT_ROWS, t), lambda i: (0, i))
        out_shape[5] = jax.ShapeDtypeStruct((MLA_VT_ROWS, n), BF16)
        out_specs[2] = pl.BlockSpec((LANES, t), lambda i: (0, i))
        out_shape[2] = jax.ShapeDtypeStruct((LANES, n), BF16)
    else:
        cache_widths = [SWA_KV_COLS, SWA_KV_COLS, MLA_KV_LORA, MLA_ROPE]
        out_specs += [row(w) for w in cache_widths]
        out_shape += [jax.ShapeDtypeStruct((n, w), F32) for w in cache_widths]
    return pl.pallas_call(
        functools.partial(_token_kernel, latent),
        grid=(n_tiles,),
        in_specs=in_specs,
        out_specs=out_specs,
        out_shape=out_shape,
        compiler_params=_cparams("parallel"),
        name="token_stage_latent" if latent else "token_stage_ctx",
    )(*args)


def _softmax_pv(scores, values, extra=None):
    m = jnp.max(scores[0], axis=-1, keepdims=True)
    for s in scores[1:]:
        m = jnp.maximum(m, jnp.max(s, axis=-1, keepdims=True))
    if extra is not None:
        m = jnp.maximum(m, extra)
    den = jnp.exp(extra - m) if extra is not None else 0.0
    acc = None
    for s, v in zip(scores, values):
        p = jnp.exp(s - m)
        den = den + jnp.sum(p, axis=-1, keepdims=True)
        pv = _dot(p.astype(BF16), v)
        acc = pv if acc is None else acc + pv
    return acc / den


def _ctx_attn_kernel(sink_ref, qa_ref, kd_ref, va_ref, qb_ref, kb_ref, vb_ref, oa_ref, ob_ref):
    rows = qa_ref.shape[0]
    low = lax.broadcasted_iota(jnp.int32, (rows, LANES), 1) < SWA_HEAD_DIM
    va = va_ref[...]
    res = []
    for hd in range(SWA_HEADS):
        qblk = qa_ref[:, (hd // 2) * LANES:(hd // 2 + 1) * LANES]
        keep = low if hd % 2 == 0 else jnp.logical_not(low)
        qm = jnp.where(keep, qblk, jnp.zeros_like(qblk))
        kvh = hd // SWA_GROUP
        s = _nt_dot(qm, kd_ref[:, kvh * LANES:(kvh + 1) * LANES]) * (SWA_HEAD_DIM ** -0.5)
        res.append(_softmax_pv([s], [va], sink_ref[hd]))
    for j in range(SWA_GROUP):
        oa_ref[:, j * LANES:(j + 1) * LANES] = jnp.where(low, res[j], res[SWA_GROUP + j]).astype(BF16)
    res = []
    for h in range(MLA_HEADS):
        sl = slice(h * HEAD_PAD, (h + 1) * HEAD_PAD)
        s = _nt_dot(qb_ref[:, sl], kb_ref[:, sl]) * (MLA_QK ** -0.5)
        res.append(_softmax_pv([s], [vb_ref[:, (h // 2) * LANES:(h // 2 + 1) * LANES]]))
    for j in range(MLA_HEADS // 2):
        ob_ref[:, j * LANES:(j + 1) * LANES] = jnp.where(low, res[2 * j], res[2 * j + 1]).astype(BF16)


def _ctx_attention(sink, qa, kd, va, qb, kb, vb, n_seq, seq):
    row = lambda w: pl.BlockSpec((seq, w), lambda b: (b, 0))
    widths = [SWA_Q_COLS, 2 * LANES, LANES, MLA_PAD_COLS, MLA_PAD_COLS, MLA_HEADS * MLA_V]
    return pl.pallas_call(
        _ctx_attn_kernel,
        grid=(n_seq,),
        in_specs=[pl.BlockSpec(memory_space=pltpu.SMEM)] + [row(w) for w in widths],
        out_specs=[row(SWA_Q_COLS), row(MLA_HEADS * MLA_V)],
        out_shape=[jax.ShapeDtypeStruct((n_seq * seq, SWA_Q_COLS), BF16),
                   jax.ShapeDtypeStruct((n_seq * seq, MLA_HEADS * MLA_V), BF16)],
        compiler_params=_cparams("parallel"),
        name="ctx_attention",
    )(sink, qa, kd, va, qb, kb, vb)


def _window_attn_kernel(sink_ref, q_ref, kp_ref, kc_ref, kn_ref, vtp_ref, vtc_ref, vtn_ref, kctx_ref,
                        vtctx_ref, o_ref):
    w = SWA_WINDOW
    i = pl.program_id(1)
    nb = pl.num_programs(1)
    cols = SWA_GROUP * w
    c = (SWA_HEAD_DIM ** -0.5) * LOG2_E
    low = lax.broadcasted_iota(jnp.int32, (w, LANES), 1) < SWA_HEAD_DIM
    m_io = lax.broadcasted_iota(jnp.int32, (3 * w, cols), 0)
    rel = m_io - (lax.broadcasted_iota(jnp.int32, (3 * w, cols), 1) & (w - 1))
    first_row = jnp.where(i == 0, w, 0)
    end_row = jnp.where(i == nb - 1, 2 * w, 3 * w)
    valid = (rel >= 0) & (rel <= 2 * w) & (m_io >= first_row) & (m_io < end_row)
    head_of_col = lax.broadcasted_iota(jnp.int32, (1, cols), 1) >> (w.bit_length() - 1)
    vt_loc = jnp.concatenate([vtp_ref[...], vtc_ref[...], vtn_ref[...]], axis=1)
    vt_ctx = vtctx_ref[...]
    top_loc = lax.broadcasted_iota(jnp.int32, vt_loc.shape, 0) < SWA_HEAD_DIM
    top_ctx = lax.broadcasted_iota(jnp.int32, vt_ctx.shape, 0) < SWA_HEAD_DIM
    staged = []
    for kvh in range(SWA_KV_HEADS):
        qs = []
        sink = jnp.zeros((1, cols), F32)
        for g in range(SWA_GROUP):
            hd = kvh * SWA_GROUP + g
            qblk = q_ref[:, (hd // 2) * LANES:(hd // 2 + 1) * LANES]
            keep = low if hd % 2 == 0 else jnp.logical_not(low)
            qs.append(jnp.where(keep, qblk, jnp.zeros_like(qblk)))
            sink = jnp.where(head_of_col == g, sink_ref[hd], sink)
        qs = jnp.concatenate(qs, axis=0)
        ks = slice(kvh * LANES, (kvh + 1) * LANES)
        k_loc = jnp.concatenate([kp_ref[:, ks], kc_ref[:, ks], kn_ref[:, ks]], axis=0)
        st_ctx = _nt_dot(kctx_ref[:, ks], qs)
        st_loc = jnp.where(valid, _nt_dot(k_loc, qs), NEG_INF)
        staged.append((st_ctx, st_loc, sink))
    res = []
    for kvh, (st_ctx, st_loc, sink) in enumerate(staged):
        top = jnp.maximum(jnp.max(st_ctx, axis=0, keepdims=True), jnp.max(st_loc, axis=0, keepdims=True))
        m2 = jnp.maximum(top * (SWA_HEAD_DIM ** -0.5), sink) * LOG2_E
        p_ctx = jnp.exp2(st_ctx * c - m2).astype(BF16)
        p_loc = jnp.exp2(st_loc * c - m2).astype(BF16)
        own_loc = top_loc if kvh == 0 else jnp.logical_not(top_loc)
        own_ctx = top_ctx if kvh == 0 else jnp.logical_not(top_ctx)
        acc = (_dot(jnp.where(own_ctx, vt_ctx, jnp.ones_like(vt_ctx)), p_ctx)
               + _dot(jnp.where(own_loc, vt_loc, jnp.ones_like(vt_loc)), p_loc))
        own = slice(kvh * SWA_HEAD_DIM, (kvh + 1) * SWA_HEAD_DIM)
        other = (1 - kvh) * SWA_HEAD_DIM
        den = acc[other:other + 1] + jnp.exp2(sink * LOG2_E - m2)
        res.append(acc[own] / den)
    for j in range(SWA_GROUP):
        cs = slice(j * w, (j + 1) * w)
        o_ref[:, j * LANES:(j + 1) * LANES] = jnp.concatenate([res[0][:, cs], res[1][:, cs]], axis=0).T.astype(BF16)


def _window_attention(sink, qa, kd, vat, kd_ctx, vt_ctx, n_seq, seq):
    w = SWA_WINDOW
    nb = seq // w
    past = kd_ctx.shape[1]
    cur = lambda b, i: (b * nb + i, 0)
    prev = lambda b, i: (b * nb + jnp.maximum(i - 1, 0), 0)
    nxt = lambda b, i: (b * nb + jnp.minimum(i + 1, nb - 1), 0)
    flip = lambda f: (lambda b, i: f(b, i)[::-1])
    return pl.pallas_call(
        _window_attn_kernel,
        grid=(n_seq, nb),
        in_specs=[
            pl.BlockSpec(memory_space=pltpu.SMEM),
            pl.BlockSpec((w, SWA_Q_COLS), cur),
            pl.BlockSpec((w, 2 * LANES), prev),
            pl.BlockSpec((w, 2 * LANES), cur),
            pl.BlockSpec((w, 2 * LANES), nxt),
            pl.BlockSpec((LANES, w), flip(prev)),
            pl.BlockSpec((LANES, w), flip(cur)),
            pl.BlockSpec((LANES, w), flip(nxt)),
            pl.BlockSpec((None, past, 2 * LANES), lambda b, i: (b, 0, 0)),
            pl.BlockSpec((None, LANES, past), lambda b, i: (b, 0, 0)),
        ],
        out_specs=pl.BlockSpec((w, SWA_Q_COLS), cur),
        out_shape=jax.ShapeDtypeStruct((n_seq * seq, SWA_Q_COLS), BF16),
        compiler_params=_cparams("parallel", "parallel"),
        name="window_attention",
    )(sink, qa, kd, kd, kd, vat, vat, vat, kd_ctx, vt_ctx)


def _latent_mla_kernel(bound_ref, q_ref, k_ref, vt_ref, kctx_ref, vtctx_ref, o_ref):
    kc = MLA_KEY_CHUNK
    c = (MLA_QK ** -0.5) * LOG2_E
    chunks = ([(kctx_ref, vtctx_ref, j) for j in range(kctx_ref.shape[0] // kc)]
              + [(k_ref, vt_ref, j) for j in range(k_ref.shape[0] // kc)])
    heads = [slice(e * HEAD_PAD, (e + 1) * HEAD_PAD) for e in range(2)]

    def scores(n):
        keys_ref, _, j = chunks[n]
        return [_nt_dot(keys_ref[j * kc:(j + 1) * kc, sl], q_ref[:, sl]) for sl in heads]

    def attend(update):
        state = [None, None]
        pending = [scores(n) for n in range(min(MLA_SCORE_LOOKAHEAD, len(chunks)))]
        for n, (_, vals_ref, j) in enumerate(chunks):
            if n + MLA_SCORE_LOOKAHEAD < len(chunks):
                pending.append(scores(n + MLA_SCORE_LOOKAHEAD))
            st = pending.pop(0)
            for e in range(2):
                vals = vals_ref[e * MLA_V_ROWS:(e + 1) * MLA_V_ROWS, j * kc:(j + 1) * kc]
                state[e] = update(state[e], st[e], vals)
        return state

    def write(accs):
        outs = [a[0:MLA_V] / a[MLA_V:MLA_V + 1] for a in accs]
        o_ref[...] = jnp.concatenate(outs, axis=0).T.astype(BF16)

    bound = bound_ref[0]

    def bounded(acc, st, vals):
        pv = _dot(vals, jnp.exp2(st * c - bound).astype(BF16))
        return pv if acc is None else acc + pv

    accs = attend(bounded)
    smallest = jnp.minimum(jnp.min(accs[0][MLA_V:MLA_V + 1]), jnp.min(accs[1][MLA_V:MLA_V + 1]))
    safe = smallest >= MLA_MIN_DENOMINATOR

    @pl.when(safe)
    def _():
        write(accs)

    @pl.when(jnp.logical_not(safe))
    def _():
        def online(state, st, vals):
            cmax = jnp.max(st, axis=0, keepdims=True)
            if state is None:
                return cmax, _dot(vals, jnp.exp2((st - cmax) * c).astype(BF16))
            m, acc = state
            m_new = jnp.maximum(m, cmax)
            p = jnp.exp2((st - m_new) * c).astype(BF16)
            return m_new, acc * jnp.exp2((m - m_new) * c) + _dot(vals, p)

        write([acc for _, acc in attend(online)])


def _latent_mla(bound, qb, kb, vb, k_ctx, v_ctx, n_seq, seq):
    tq = MLA_Q_TILE
    nq = seq // tq
    past = k_ctx.shape[1]
    pair = 2 * HEAD_PAD
    assert past % MLA_KEY_CHUNK == 0 and seq % MLA_KEY_CHUNK == 0
    return pl.pallas_call(
        _latent_mla_kernel,
        grid=(n_seq, MLA_HEADS // 2, nq),
        in_specs=[
            pl.BlockSpec(memory_space=pltpu.SMEM),
            pl.BlockSpec((tq, pair), lambda b, p, i: (b * nq + i, p)),
            pl.BlockSpec((seq, pair), lambda b, p, i: (b, p)),
            pl.BlockSpec((2 * MLA_V_ROWS, seq), lambda b, p, i: (p, b)),
            pl.BlockSpec((None, past, pair), lambda b, p, i: (b, 0, p)),
            pl.BlockSpec((None, 2 * MLA_V_ROWS, past), lambda b, p, i: (b, p, 0)),
        ],
        out_specs=pl.BlockSpec((tq, LANES), lambda b, p, i: (b * nq + i, p)),
        out_shape=jax.ShapeDtypeStruct((n_seq * seq, MLA_HEADS * MLA_V), BF16),
        compiler_params=_cparams("parallel", "parallel", "parallel"),
        name="latent_mla",
    )(bound, qb, kb, vb, k_ctx, v_ctx)


def _route(logits):
    lane = lax.broadcasted_iota(jnp.int32, logits.shape, 1).astype(F32)
    big = float(4 * LANES)
    gmask = lane < N_EXPERT_GROUPS
    gmax = jnp.max(jnp.where(gmask, logits, -jnp.inf), axis=-1, keepdims=True)
    gsum = jnp.sum(jnp.where(gmask, jnp.exp(logits - gmax), 0.0), axis=-1, keepdims=True)
    g_w = 1.0 / gsum
    g_idx = jnp.min(jnp.where(gmask & (logits == gmax), lane, big), axis=-1, keepdims=True)
    first = ROUTER_LANE0 + EXPERTS_PER_GROUP * g_idx
    emask = (lane >= first) & (lane < first + EXPERTS_PER_GROUP)
    el = jnp.where(emask, logits, -jnp.inf)
    m1 = jnp.max(el, axis=-1, keepdims=True)
    i1 = jnp.min(jnp.where(emask & (el == m1), lane, big), axis=-1, keepdims=True)
    el2 = jnp.where(lane == i1, -jnp.inf, el)
    m2 = jnp.max(el2, axis=-1, keepdims=True)
    i2 = jnp.min(jnp.where(emask & (el2 == m2), lane, big), axis=-1, keepdims=True)
    esum = jnp.sum(jnp.where(emask, jnp.exp(logits - m1), 0.0), axis=-1, keepdims=True)
    p1 = 1.0 / esum
    p2 = jnp.exp(m2 - m1) / esum
    tot = p1 + p2
    combine = jnp.where(lane == i1, g_w * (p1 / tot), 0.0) + jnp.where(lane == i2, g_w * (p2 / tot), 0.0)
    return jnp.where(lane == g_idx, 1.0, combine)


def _finish_kernel(x_ref, oa_ref, ob_ref, oc_ref, mod_ref, gmix_ref, gffn_ref, wg_ref, woa_ref, wob_ref,
                   woc_ref, wout_ref, wrh_ref, wrl_ref, br_ref, x1_ref, h2_ref, comb_ref):
    d = D_MODEL
    x = x_ref[...]
    h = _modulated_norm(x, gmix_ref[...], mod_ref[:, 0:d], mod_ref[:, d:2 * d]).astype(BF16)
    gates = _dot(h, wg_ref[...])
    merged = (jax.nn.sigmoid(gates[:, 0:d]) * _dot(oa_ref[...], woa_ref[...])
              + jax.nn.sigmoid(gates[:, d:2 * d]) * _dot(ob_ref[...], wob_ref[...])
              + jax.nn.sigmoid(gates[:, 2 * d:3 * d]) * _dot(oc_ref[...], woc_ref[...]))
    x1 = x + mod_ref[:, 2 * d:3 * d] * _dot(merged.astype(BF16), wout_ref[...])
    x1_ref[...] = x1
    h2 = _modulated_norm(x1, gffn_ref[...], mod_ref[:, 3 * d:4 * d], mod_ref[:, 4 * d:5 * d])
    h2_hi = h2.astype(BF16)
    h2_lo = (h2 - h2_hi.astype(F32)).astype(BF16)
    h2_ref[...] = h2_hi
    logits = (_dot(h2_hi, wrh_ref[...]) + _dot(h2_lo, wrh_ref[...]) + _dot(h2_hi, wrl_ref[...])) + br_ref[...]
    comb_ref[...] = _route(logits)


def _finish(x_all, oa, ob, oc, mod_l, mod_row, wl):
    t = TOKEN_TILE
    d = D_MODEL
    n = x_all.shape[0]
    full = lambda a: pl.BlockSpec(a.shape, lambda i: (0,) * a.ndim)
    row = lambda w: pl.BlockSpec((t, w), lambda i: (i, 0))
    weights = [wl["g_mix"], wl["g_ffn"], wl["w_g"], wl["w_o_a"], wl["w_o_b"], wl["w_o_c"], wl["w_out"],
               wl["w_r_hi"], wl["w_r_lo"], wl["b_r"]]
    return pl.pallas_call(
        _finish_kernel,
        grid=(n // t,),
        in_specs=[row(d), row(SWA_Q_COLS), row(MLA_HEADS * MLA_V), row(GM_WIDTH),
                  pl.BlockSpec((None, 1, 6 * d), lambda i: (mod_row(i), 0, 0))] + [full(a) for a in weights],
        out_specs=[row(d), row(d), row(LANES)],
        out_shape=[jax.ShapeDtypeStruct((n, d), F32), jax.ShapeDtypeStruct((n, d), BF16),
                   jax.ShapeDtypeStruct((n, LANES), F32)],
        compiler_params=_cparams("parallel"),
        name="finish",
    )(x_all, oa, ob, oc, mod_l, *weights)


def _moe_kernel(h_ref, route_ref, x1_ref, mod_ref, tri_ref, wg_ref, wu_ref, wd_ref, o_ref,
                xs, rs, ys, dest_cols, windows):
    d = D_MODEL
    t = h_ref.shape[0]
    ch = MOE_CHUNK
    blk = MXU_DIM
    s = pl.program_id(1)
    steps_per_group = EXPERTS_PER_GROUP // MOE_EXPERTS_PER_STEP
    g = s // steps_per_group
    j0 = (s % steps_per_group) * MOE_EXPERTS_PER_STEP

    @pl.when(s == 0)
    def _():
        route = route_ref[...]
        rt = route.T
        rank = _dot(rt.astype(BF16), tri_ref[...])
        rank_c = rank.T
        lane1 = lax.broadcasted_iota(jnp.int32, (1, LANES), 1)
        dest_row = jnp.zeros((1, t), F32)
        start_vec = jnp.zeros((1, LANES), F32)
        start = jnp.zeros((1, 1), F32)
        for gg in range(N_EXPERT_GROUPS):
            ind = rt[gg:gg + 1, :]
            count = jnp.sum(ind, axis=-1, keepdims=True)
            dest_row = dest_row + ind * (rank[gg:gg + 1, :] + start)
            start_vec = jnp.where(lane1 == gg, start, start_vec)
            first = jnp.sum(start).astype(jnp.int32)
            base = (first // BF16_SUBLANES) * BF16_SUBLANES
            windows[2 * gg] = base
            windows[2 * gg + 1] = (first - base + jnp.sum(count).astype(jnp.int32) + (ch - 1)) // ch
            start = start + count
        lane_t = lax.broadcasted_iota(jnp.int32, (t, LANES), 1)
        dest_cols[...] = jnp.sum(jnp.where(lane_t < N_EXPERT_GROUPS, route * (rank_c + start_vec), 0.0),
                                 axis=-1, keepdims=True)
        hi = route.astype(BF16)
        split = jnp.concatenate([hi, (route - hi.astype(F32)).astype(BF16)], axis=1)
        for b in range(t // blk):
            r_io = (lax.broadcasted_iota(jnp.int32, (blk, t), 0) + b * blk).astype(F32)
            perm = jnp.where(dest_row == r_io, 1.0, 0.0).astype(BF16)
            xs[b * blk:(b + 1) * blk, :] = _dot(perm, h_ref[...]).astype(BF16)
            r2 = _dot(perm, split)
            rs[b * blk:(b + 1) * blk, :] = r2[:, 0:LANES] + r2[:, LANES:2 * LANES]
        xs[t:, :] = jnp.zeros((xs.shape[0] - t, d), BF16)
        rs[t:, :] = jnp.zeros((rs.shape[0] - t, LANES), F32)
        ys[...] = jnp.zeros_like(ys)

    base = windows[2 * g]
    n_windows = windows[2 * g + 1]

    def ffn(w, carry):
        rows = pl.ds(pl.multiple_of(base + w * ch, BF16_SUBLANES), ch)
        x = xs[rows, :]
        wt = rs[rows, :]
        lane = lax.broadcasted_iota(jnp.int32, (ch, LANES), 1)
        total = None
        for k in range(MOE_EXPERTS_PER_STEP):
            a = _dot(x, wg_ref[k])
            hid = (a * jax.nn.sigmoid(a)) * _dot(x, wu_ref[k])
            y = _dot(hid.astype(BF16), wd_ref[k])
            w_col = jnp.sum(jnp.where(lane == ROUTER_LANE0 + EXPERTS_PER_GROUP * g + j0 + k, wt, 0.0), axis=-1,
                            keepdims=True)
            total = w_col * y if total is None else total + w_col * y
        ys[rows, :] += total
        return carry

    lax.fori_loop(0, n_windows, ffn, 0)

    @pl.when(s == pl.num_programs(1) - 1)
    def _():
        dest_col = dest_cols[...]
        for b in range(t // blk):
            c_io = (lax.broadcasted_iota(jnp.int32, (t, blk), 1) + b * blk).astype(F32)
            perm_t = jnp.where(dest_col == c_io, 1.0, 0.0).astype(BF16)
            part = _dot(perm_t, ys[b * blk:(b + 1) * blk, :].astype(BF16))
            if b == 0:
                o_ref[...] = part
            else:
                o_ref[...] += part
        o_ref[...] = x1_ref[...] + mod_ref[:, 5 * d:6 * d] * o_ref[...]


def _moe(h2, route, x1, mod_l, mod_row, wl, tri):
    d = D_MODEL
    n = h2.shape[0]
    tile = tri.shape[0]
    sorted_rows = tile + MOE_CHUNK
    eps = MOE_EXPERTS_PER_STEP
    row = lambda w: pl.BlockSpec((tile, w), lambda i, e: (i, 0))
    return pl.pallas_call(
        _moe_kernel,
        grid=(n // tile, N_EXPERTS // eps),
        in_specs=[row(d), row(LANES), row(d),
                  pl.BlockSpec((None, 1, 6 * d), lambda i, e: (mod_row(i), 0, 0)),
                  pl.BlockSpec((tile, tile), lambda i, e: (0, 0)),
                  pl.BlockSpec((eps, d, EXPERT_FF), lambda i, e: (e, 0, 0)),
                  pl.BlockSpec((eps, d, EXPERT_FF), lambda i, e: (e, 0, 0)),
                  pl.BlockSpec((eps, EXPERT_FF, d), lambda i, e: (e, 0, 0))],
        out_specs=row(d),
        out_shape=jax.ShapeDtypeStruct((n, d), F32),
        scratch_shapes=[pltpu.VMEM((sorted_rows, d), BF16), pltpu.VMEM((sorted_rows, LANES), F32),
                        pltpu.VMEM((sorted_rows, d), F32), pltpu.VMEM((tile, 1), F32),
                        pltpu.SMEM((2 * N_EXPERT_GROUPS,), jnp.int32)],
        compiler_params=_cparams("parallel", "arbitrary"),
        name="moe_grouped",
    )(h2, route, x1, mod_l, tri, wl["w_gate"], wl["w_up"], wl["w_down"])


def _block_ones(group):
    idx = np.arange(MXU_DIM)
    return jnp.asarray((idx[:, None] // group) == (idx[None, :] // group), BF16)


def _rotate_half_matrix(block, rot_start, rot_dim):
    half = rot_dim // 4
    r = np.zeros((MXU_DIM, MXU_DIM), np.float32)
    for j in range(MXU_DIM):
        o = j % block - rot_start
        if 0 <= o < rot_dim:
            partner = j + half if (o % (2 * half)) < half else j - half
            r[partner, j] = 1.0
    return jnp.asarray(r, BF16)


def _rope_tables(n_tokens, block, rot_start, rot_dim):
    half = rot_dim // 4
    tok = np.arange(n_tokens)
    pos = np.stack([tok // GRID_W, tok % GRID_W], axis=1).astype(np.float32)
    inv_freq = (ROPE_THETA ** (-np.arange(half, dtype=np.float32) / half)).astype(np.float32)
    lane = np.arange(LANES)
    o = lane % block - rot_start
    rot = (o >= 0) & (o < rot_dim)
    oc = np.where(rot, o, 0)
    axis = oc // (2 * half)
    freq = inv_freq[oc % half]
    sign = np.where((oc % (2 * half)) < half, -1.0, 1.0).astype(np.float32)
    ang = jnp.asarray(pos[:, axis]) * jnp.asarray(freq)[None, :]
    rot_j = jnp.asarray(rot)[None, :]
    cos = jnp.where(rot_j, jnp.cos(ang), 1.0)
    sin = jnp.where(rot_j, jnp.sin(ang) * jnp.asarray(sign)[None, :], 0.0)
    return cos.astype(F32), sin.astype(F32)


def _pad_heads(w, heads, width, padded=HEAD_PAD):
    lead = w.shape[:-1]
    w = w.reshape(lead + (heads, width))
    w = jnp.pad(w, [(0, 0)] * len(lead) + [(0, 0), (0, padded - width)])
    return w.reshape(lead + (heads * padded,))


def _prep_weights(p):
    depth = p["w_in"].shape[0]
    w_in = p["w_in"]
    c = np.cumsum((0, SWA_Q_COLS, SWA_KV_COLS, SWA_KV_COLS, MLA_Q_LORA, MLA_KV_LORA, MLA_ROPE, GM_WIDTH,
                   GM_WIDTH))
    kr = w_in[..., c[5]:c[6]]
    zero = jnp.zeros_like(kr)
    w_a = jnp.concatenate([w_in[..., c[0]:c[5]], kr, zero, kr, zero, w_in[..., c[6]:c[8]]], axis=-1)
    ukv = p["mla_w_ukv"].reshape(depth, MLA_KV_LORA, MLA_HEADS, MLA_NOPE + MLA_V)
    order = np.array([0, 4, 1, 5, 2, 6, 3, 7])
    w_o_a = p["w_o_a"].reshape(depth, SWA_HEADS, SWA_HEAD_DIM, D_MODEL)[:, order].reshape(depth, SWA_Q_COLS,
                                                                                          D_MODEL)
    w_r = jnp.concatenate([p["w_rg"], p["w_re"]], axis=-1)
    w_r = jnp.pad(w_r, ((0, 0), (0, 0), (0, LANES - w_r.shape[-1])))
    w_r_hi = w_r.astype(BF16)
    b_r = jnp.concatenate([p["b_rg"], p["b_re"]], axis=-1)
    b_r = jnp.pad(b_r, ((0, 0), (0, LANES - b_r.shape[-1])))
    k_gain = p["mla_k_norm"]
    row = lambda a: a.reshape(depth, 1, a.shape[-1]).astype(F32)
    bs_full = jnp.repeat(jnp.swapaxes(p["gm_b_s"], 1, 2), LANES, axis=-1)
    return {
        "g_mix": row(p["g_mix"]), "g_ffn": row(p["g_ffn"]),
        "w_a": w_a.astype(BF16), "w_g": w_in[..., _GATE0:].astype(BF16),
        "gqa": row(jnp.tile(p["swa_q_norm"], (1, SWA_HEADS))),
        "gka": row(jnp.tile(p["swa_k_norm"], (1, SWA_KV_HEADS))),
        "gcq": row(p["mla_cq_norm"]), "gckv": row(p["mla_ckv_norm"]),
        "w_uq": _pad_heads(p["mla_w_uq"], MLA_HEADS, MLA_QK).astype(BF16),
        "gq": row(_pad_heads(jnp.tile(p["mla_q_norm"], (1, MLA_HEADS)), MLA_HEADS, MLA_QK)),
        "w_kn": _pad_heads(ukv[..., :MLA_NOPE].reshape(depth, MLA_KV_LORA, -1), MLA_HEADS, MLA_NOPE).astype(BF16),
        "w_v": ukv[..., MLA_NOPE:].reshape(depth, MLA_KV_LORA, -1).astype(BF16),
        "w_vp": _pad_heads(ukv[..., MLA_NOPE:].reshape(depth, MLA_KV_LORA, -1), MLA_HEADS, MLA_V,
                           MLA_V_ROWS).astype(BF16),
        "gk1": row(_pad_heads(jnp.tile(k_gain[:, :MLA_NOPE], (1, MLA_HEADS)), MLA_HEADS, MLA_NOPE)),
        "gk2": row(jnp.pad(k_gain[:, MLA_NOPE:], ((0, 0), (MLA_NOPE, LANES - MLA_QK)))),
        "ln_g": row(p["gm_ln_g"]), "ln_b": row(p["gm_ln_b"]),
        "w_s": p["gm_w_s"].astype(BF16), "b_s": bs_full.astype(F32),
        "w_o_a": w_o_a.astype(BF16), "w_o_b": p["w_o_b"].astype(BF16), "w_o_c": p["w_o_c"].astype(BF16),
        "w_out": p["w_out"].astype(BF16),
        "w_r_hi": w_r_hi, "w_r_lo": (w_r - w_r_hi.astype(F32)).astype(BF16), "b_r": row(b_r),
        "w_gate": p["w_gate"].astype(BF16), "w_up": p["w_up"].astype(BF16), "w_down": p["w_down"].astype(BF16),
    }


def kernel(x_prompt, x_sample, cache_swa_k, cache_swa_v, cache_mla_ckv, cache_mla_krope, c, c_ctx, w_mod, b_mod, g_mix, g_ffn, w_in, swa_q_norm, swa_k_norm, swa_sink, mla_cq_norm, mla_ckv_norm, mla_w_uq, mla_w_ukv, mla_q_norm, mla_k_norm, gm_ln_g, gm_ln_b, gm_w_s, gm_b_s, w_o_a, w_o_b, w_o_c, w_out, w_rg, b_rg, w_re, b_re, w_gate, w_up, w_down):
    d = D_MODEL
    n_ctx_seq, ctx_len, _ = x_prompt.shape
    n_lat_seq, lat_len, _ = x_sample.shape
    depth = w_in.shape[0]
    past = cache_swa_k.shape[2]
    n_ctx = n_ctx_seq * ctx_len
    n_lat = n_lat_seq * lat_len
    t = TOKEN_TILE
    assert n_ctx % t == 0 and ctx_len % GM_CHUNK == 0 and lat_len % t == 0 and lat_len % MLA_Q_TILE == 0 and lat_len % GRID_W == 0 and n_lat_seq < 8
    ctx_tiles = n_ctx // t
    lat_tiles = n_lat // t
    moe_tile = next(m for m in MOE_TILE_CANDIDATES if n_ctx % m == 0 and lat_len % m == 0)

    params = dict(w_in=w_in, g_mix=g_mix, g_ffn=g_ffn, swa_q_norm=swa_q_norm, swa_k_norm=swa_k_norm,
                  mla_cq_norm=mla_cq_norm, mla_ckv_norm=mla_ckv_norm, mla_w_uq=mla_w_uq, mla_w_ukv=mla_w_ukv,
                  mla_q_norm=mla_q_norm, mla_k_norm=mla_k_norm, gm_ln_g=gm_ln_g, gm_ln_b=gm_ln_b,
                  gm_w_s=gm_w_s, gm_b_s=gm_b_s, w_o_a=w_o_a, w_o_b=w_o_b, w_o_c=w_o_c, w_out=w_out,
                  w_rg=w_rg, b_rg=b_rg, w_re=w_re, b_re=b_re, w_gate=w_gate, w_up=w_up, w_down=w_down)
    w_all = _prep_weights(params)
    consts = {
        "ones64": _block_ones(SWA_HEAD_DIM), "ones128": _block_ones(HEAD_PAD),
        "rot_a": _rotate_half_matrix(SWA_HEAD_DIM, 0, SWA_HEAD_DIM),
        "rot_b": _rotate_half_matrix(HEAD_PAD, MLA_NOPE, MLA_ROPE),
        "v_ones": jnp.asarray((np.arange(MLA_VT_ROWS) % MLA_V_ROWS >= MLA_V).astype(np.float32)[None, :]),
        "tri": jnp.asarray(np.triu(np.ones((moe_tile, moe_tile), np.float32), k=1), BF16),
    }
    cos_a, sin_a = _rope_tables(lat_len, SWA_HEAD_DIM, 0, SWA_HEAD_DIM)
    cos_b, sin_b = _rope_tables(lat_len, HEAD_PAD, MLA_NOPE, MLA_ROPE)
    tables = {"cos_a": cos_a, "sin_a": sin_a, "cos_b": cos_b, "sin_b": sin_b}

    cond8 = jnp.zeros((8, d), F32).at[:n_lat_seq].set(c).at[n_lat_seq].set(c_ctx)
    mods = _adaln(cond8, w_mod, b_mod).reshape(depth, 8, 1, 6 * d)

    k0 = cache_swa_k[:, :, :, 0, :]
    k1 = cache_swa_k[:, :, :, 1, :]
    kd_ctx_all = jnp.concatenate([k0, k0, k1, k1], axis=-1).astype(BF16)
    vt_swa_ctx_all = jnp.swapaxes(cache_swa_v.reshape(n_lat_seq, depth, past, SWA_KV_COLS), 2, 3).astype(BF16)
    kr_hi = jnp.pad(cache_mla_krope, ((0, 0), (0, 0), (0, 0), (MLA_NOPE, LANES - MLA_QK)))
    kb_ctx_all, vt_ctx_all = _ctx_keys(cache_mla_ckv, kr_hi, w_all["w_kn"], w_all["w_vp"], w_all["gk1"],
                                       w_all["gk2"], consts["v_ones"], consts["ones128"])

    ctx_row = lambda i: n_lat_seq
    lat_row = lambda i: i // (lat_len // t)
    lat_moe_row = lambda i: i // (lat_len // moe_tile)

    x_c = x_prompt.reshape(n_ctx, d)
    x_s = x_sample.reshape(n_lat, d)
    caches = [[], [], [], []]
    for l in range(depth):
        wl = {k: v[l] for k, v in w_all.items()}
        mod_l = mods[l]
        sink = swa_sink[l].reshape(SWA_HEADS)
        (qa_c, kd_c, va_c, qb_c, kb_c, vb_c, oc_c, ka_f, va_f, ckv_f, kr_f) = _token_stage(
            False, x_c, ctx_tiles, mod_l, ctx_row, wl, consts, None)
        (qa_s, kd_s, va_s, qb_s, kb_s, vb_s, oc_s) = _token_stage(
            True, x_s, lat_tiles, mod_l, lat_row, wl, consts, tables)
        for dst, val in zip(caches, (ka_f, va_f, ckv_f, kr_f)):
            dst.append(val)
        oa_c, ob_c = _ctx_attention(sink, qa_c, kd_c, va_c, qb_c, kb_c, vb_c, n_ctx_seq, ctx_len)
        oa_s = _window_attention(sink, qa_s, kd_s, va_s, kd_ctx_all[:, l], vt_swa_ctx_all[:, l], n_lat_seq,
                                 lat_len)
        mla_bound = (MLA_QK * jnp.max(jnp.abs(mla_q_norm[l])) * jnp.max(jnp.abs(mla_k_norm[l]))
                     * (MLA_QK ** -0.5) * LOG2_E).reshape(1).astype(F32)
        ob_s = _latent_mla(mla_bound, qb_s, kb_s, vb_s, kb_ctx_all[l], vt_ctx_all[l], n_lat_seq, lat_len)
        x1_c, h2_c, route_c = _finish(x_c, oa_c, ob_c, oc_c, mod_l, ctx_row, wl)
        x1_s, h2_s, route_s = _finish(x_s, oa_s, ob_s, oc_s, mod_l, lat_row, wl)
        x_c = _moe(h2_c, route_c, x1_c, mod_l, ctx_row, wl, consts["tri"])
        x_s = _moe(h2_s, route_s, x1_s, mod_l, lat_moe_row, wl, consts["tri"])

    y_prompt = x_c.reshape(n_ctx_seq, ctx_len, d)
    y_sample = x_s.reshape(n_lat_seq, lat_len, d)
    stack = lambda vals, tail: jnp.stack([v.reshape((n_ctx_seq, ctx_len) + tail) for v in vals], axis=1)
    return (y_prompt, y_sample,
            stack(caches[0], (SWA_KV_HEADS, SWA_HEAD_DIM)), stack(caches[1], (SWA_KV_HEADS, SWA_HEAD_DIM)),
            stack(caches[2], (MLA_KV_LORA,)), stack(caches[3], (MLA_ROPE,)))
```

```python
import functools

import numpy as np
import jax
import jax.numpy as jnp
from jax import lax
from jax.experimental import pallas as pl
from jax.experimental.pallas import tpu as pltpu

F32 = jnp.float32
BF16 = jnp.bfloat16

D_MODEL = 1024
GRID_W = 64
ROPE_THETA = 10000.0
EPS = 1e-6
NEG_INF = -1e30

SWA_HEADS = 8
SWA_KV_HEADS = 2
SWA_GROUP = SWA_HEADS // SWA_KV_HEADS
SWA_HEAD_DIM = 64
SWA_WINDOW = 128
SWA_Q_COLS = SWA_HEADS * SWA_HEAD_DIM
SWA_KV_COLS = SWA_KV_HEADS * SWA_HEAD_DIM

MLA_HEADS = 8
MLA_Q_LORA = 256
MLA_KV_LORA = 128
MLA_NOPE = 64
MLA_ROPE = 32
MLA_V = 64
MLA_QK = MLA_NOPE + MLA_ROPE

GM_CHUNK = 128
GM_GROUPS = 4
GM_WIDTH = 512

N_EXPERT_GROUPS = 4
EXPERTS_PER_GROUP = 4
N_EXPERTS = N_EXPERT_GROUPS * EXPERTS_PER_GROUP
EXPERT_FF = 512

LANES = 128
MXU_DIM = 256
VMEM_LIMIT_BYTES = 56 * 1024 * 1024

HEAD_PAD = LANES
MLA_PAD_COLS = MLA_HEADS * HEAD_PAD
BF16_SUBLANES = 16
MLA_V_ROWS = HEAD_PAD
MLA_VT_ROWS = MLA_HEADS * MLA_V_ROWS

TOKEN_TILE = 512
MLA_Q_TILE = 512
MLA_KEY_CHUNK = 512
MLA_SCORE_LOOKAHEAD = 2
MLA_MIN_DENOMINATOR = 2.0 ** -90
LOG2_E = float(np.log2(np.e))
MOE_TILE_CANDIDATES = (1024, 512, 256)
MOE_CHUNK = 320
MOE_EXPERTS_PER_STEP = 2

_QA0, _KA0, _VA0, _CQ0, _CKV0, _KR0, _U0, _V0, _ZCOLS = 0, 512, 640, 768, 1024, 1152, 1280, 1792, 2304
_GATE0 = SWA_Q_COLS + 2 * SWA_KV_COLS + MLA_Q_LORA + MLA_KV_LORA + MLA_ROPE + 2 * GM_WIDTH

ROUTER_LANE0 = N_EXPERT_GROUPS


def _cparams(*sem):
    return pltpu.CompilerParams(dimension_semantics=sem, vmem_limit_bytes=VMEM_LIMIT_BYTES)


def _nt_dot(a, b):
    return lax.dot_general(a, b, (((1,), (1,)), ((), ())), preferred_element_type=F32)


def _dot(a, b):
    return jnp.dot(a, b, preferred_element_type=F32)


def _gelu_tanh(x):
    return 0.5 * x * (1.0 + jnp.tanh(np.sqrt(2.0 / np.pi) * (x + 0.044715 * (x * x * x))))


def _group_sumsq(v, ones_ref):
    width = ones_ref.shape[0]
    parts = []
    for c in range(v.shape[1] // width):
        blk = v[:, c * width:(c + 1) * width]
        parts.append(_dot((blk * blk).astype(BF16), ones_ref[...]))
    return parts[0] if len(parts) == 1 else jnp.concatenate(parts, axis=1)


def _rope(v, cos, sin, rot_ref):
    width = rot_ref.shape[0]
    parts = []
    for c in range(v.shape[1] // width):
        blk = v[:, c * width:(c + 1) * width]
        rot = _dot(blk.astype(BF16), rot_ref[...])
        for s in range(width // LANES):
            sl = slice(s * LANES, (s + 1) * LANES)
            parts.append(blk[:, sl] * cos + rot[:, sl] * sin)
    return parts[0] if len(parts) == 1 else jnp.concatenate(parts, axis=1)


def _modulated_norm(x, gain, shift, scale):
    ms = jnp.mean(x * x, axis=-1, keepdims=True)
    h = x * lax.rsqrt(ms + EPS) * gain
    return h * (1.0 + scale) + shift


def _adaln_kernel(cond_ref, w_ref, b_ref, o_ref):
    a = cond_ref[...]
    a = a * jax.nn.sigmoid(a)
    o_ref[...] = _dot(a.astype(BF16), w_ref[...].astype(BF16)) + b_ref[...]


def _adaln(cond8, w_mod, b_mod):
    depth, d, n = w_mod.shape
    tn = 1536
    return pl.pallas_call(
        _adaln_kernel,
        grid=(depth, n // tn),
        in_specs=[
            pl.BlockSpec((8, d), lambda l, j: (0, 0)),
            pl.BlockSpec((None, d, tn), lambda l, j: (l, 0, j)),
            pl.BlockSpec((None, 1, tn), lambda l, j: (l, 0, j)),
        ],
        out_specs=pl.BlockSpec((None, 8, tn), lambda l, j: (l, 0, j)),
        out_shape=jax.ShapeDtypeStruct((depth, 8, n), F32),
        compiler_params=_cparams("parallel", "parallel"),
        name="adaln",
    )(cond8, w_mod, b_mod.reshape(depth, 1, n))


def _mla_keys_values(ckv_n, kr_hi, wkn_ref, wv_ref, gk1_ref, gk2_ref, ones128_ref, rope_args):
    cb = ckv_n.astype(BF16)
    nope = _dot(cb, wkn_ref[...])
    vals = _dot(cb, wv_ref[...])
    kr_ss = jnp.sum(kr_hi * kr_hi, axis=-1, keepdims=True)
    inv = lax.rsqrt((_group_sumsq(nope, ones128_ref) + kr_ss) * (1.0 / MLA_QK) + EPS)
    krg = kr_hi * gk2_ref[...]
    if rope_args is not None:
        cos, sin, rot_ref = rope_args
        krg = _rope(krg, cos, sin, rot_ref)
    gk1 = gk1_ref[...]
    parts = []
    for h in range(MLA_HEADS):
        sl = slice(h * HEAD_PAD, (h + 1) * HEAD_PAD)
        parts.append((nope[:, sl] * gk1[:, sl] + krg) * inv[:, sl])
    return jnp.concatenate(parts, axis=1), vals


def _values_with_sum_rows(vals_pad, vone_ref):
    return (vals_pad + vone_ref[...]).T.astype(BF16)


def _ctx_keys_kernel(ckv_ref, kr_ref, wkn_ref, wv_ref, gk1_ref, gk2_ref, vone_ref, ones128_ref, k_ref, vt_ref):
    keys, vals = _mla_keys_values(ckv_ref[...], kr_ref[...], wkn_ref, wv_ref, gk1_ref, gk2_ref,
                                  ones128_ref, None)
    k_ref[...] = keys.astype(BF16)
    vt_ref[...] = _values_with_sum_rows(vals, vone_ref)


def _ctx_keys(cache_ckv, cache_kr_hi, wkn, wvp, gk1, gk2, vone, ones128):
    nb, depth, past, _ = cache_ckv.shape
    cache_map = lambda l, b: (b, l, 0, 0)
    w_map = lambda l, b: (l, 0, 0)
    return pl.pallas_call(
        _ctx_keys_kernel,
        grid=(depth, nb),
        in_specs=[
            pl.BlockSpec((None, None, past, MLA_KV_LORA), cache_map),
            pl.BlockSpec((None, None, past, LANES), cache_map),
            pl.BlockSpec((None, MLA_KV_LORA, MLA_PAD_COLS), w_map),
            pl.BlockSpec((None, MLA_KV_LORA, MLA_VT_ROWS), w_map),
            pl.BlockSpec((None, 1, MLA_PAD_COLS), w_map),
            pl.BlockSpec((None, 1, LANES), w_map),
            pl.BlockSpec((1, MLA_VT_ROWS), lambda l, b: (0, 0)),
            pl.BlockSpec((MXU_DIM, MXU_DIM), lambda l, b: (0, 0)),
        ],
        out_specs=[
            pl.BlockSpec((None, None, past, MLA_PAD_COLS), lambda l, b: (l, b, 0, 0)),
            pl.BlockSpec((None, None, MLA_VT_ROWS, past), lambda l, b: (l, b, 0, 0)),
        ],
        out_shape=[
            jax.ShapeDtypeStruct((depth, nb, past, MLA_PAD_COLS), BF16),
            jax.ShapeDtypeStruct((depth, nb, MLA_VT_ROWS, past), BF16),
        ],
        compiler_params=_cparams("parallel", "parallel"),
        name="ctx_keys",
    )(cache_ckv, cache_kr_hi, wkn, wvp, gk1, gk2, vone, ones128)


def _token_kernel(latent, x_ref, mod_ref, gmix_ref, wa_ref, gqa_ref, gka_ref, gcq_ref, wuq_ref, gq_ref,
                  gckv_ref, wkn_ref, wv_ref, gk1_ref, gk2_ref, lng_ref, lnb_ref, ws_ref, bs_ref,
                  ones64_ref, ones128_ref, *rest):
    if latent:
        (cosa_ref, sina_ref, cosb_ref, sinb_ref, rota_ref, rotb_ref, vone_ref,
         qa_o, kd_o, va_o, qb_o, kb_o, vb_o, oc_o) = rest
    else:
        (qa_o, kd_o, va_o, qb_o, kb_o, vb_o, oc_o, kac_o, vac_o, ckvc_o, krc_o) = rest
    d = D_MODEL
    rows = x_ref.shape[0]
    h = _modulated_norm(x_ref[...], gmix_ref[...], mod_ref[:, 0:d], mod_ref[:, d:2 * d])
    z = _dot(h.astype(BF16), wa_ref[...])

    qa = z[:, _QA0:_KA0]
    qa = qa * lax.rsqrt(_group_sumsq(qa, ones64_ref) * (1.0 / SWA_HEAD_DIM) + EPS) * gqa_ref[...]
    ka = z[:, _KA0:_VA0]
    ka_ss = _dot((ka * ka).astype(BF16), ones64_ref[0:LANES, 0:LANES])
    ka = ka * lax.rsqrt(ka_ss * (1.0 / SWA_HEAD_DIM) + EPS) * gka_ref[...]
    va = z[:, _VA0:_CQ0]
    if latent:
        qa = _rope(qa, cosa_ref[...], sina_ref[...], rota_ref)
        ka_r = _rope(ka, cosa_ref[...], sina_ref[...], rota_ref.at[0:LANES, 0:LANES])
    else:
        kac_o[...] = ka
        vac_o[...] = va
        ka_r = ka
    swapped = pltpu.roll(ka_r, SWA_HEAD_DIM, 1)
    low = lax.broadcasted_iota(jnp.int32, (rows, LANES), 1) < SWA_HEAD_DIM
    qa_o[...] = qa.astype(BF16)
    kd_o[:, 0:LANES] = jnp.where(low, ka_r, swapped).astype(BF16)
    kd_o[:, LANES:2 * LANES] = jnp.where(low, swapped, ka_r).astype(BF16)
    va_o[...] = va.T.astype(BF16) if latent else va.astype(BF16)

    cq = z[:, _CQ0:_CKV0]
    cq = cq * lax.rsqrt(jnp.mean(cq * cq, axis=-1, keepdims=True) + EPS) * gcq_ref[...]
    qb = _dot(cq.astype(BF16), wuq_ref[...])
    qb = qb * lax.rsqrt(_group_sumsq(qb, ones128_ref) * (1.0 / MLA_QK) + EPS) * gq_ref[...]
    if latent:
        qb = _rope(qb, cosb_ref[...], sinb_ref[...], rotb_ref)
    qb_o[...] = qb.astype(BF16)

    ckv = z[:, _CKV0:_KR0]
    ckv = ckv * lax.rsqrt(jnp.mean(ckv * ckv, axis=-1, keepdims=True) + EPS) * gckv_ref[...]
    krb = z[:, _KR0:_U0]
    lane = lax.broadcasted_iota(jnp.int32, (rows, LANES), 1)
    kr_hi = jnp.where(lane >= MLA_NOPE, krb, 0.0)
    rope_args = (cosb_ref[...], sinb_ref[...], rotb_ref.at[0:LANES, 0:LANES]) if latent else None
    kb, vb = _mla_keys_values(ckv, kr_hi, wkn_ref, wv_ref, gk1_ref, gk2_ref, ones128_ref, rope_args)
    kb_o[...] = kb.astype(BF16)
    if latent:
        vb_o[...] = _values_with_sum_rows(vb, vone_ref)
    else:
        vb_o[...] = vb.astype(BF16)
    if not latent:
        ckvc_o[...] = ckv
        krc_o[...] = krb[:, 0:MLA_ROPE]

    gu = _gelu_tanh(z[:, _U0:_V0])
    gv = _gelu_tanh(z[:, _V0:_ZCOLS])
    gc = gv - jnp.mean(gv, axis=-1, keepdims=True)
    var = jnp.mean(gc * gc, axis=-1, keepdims=True)
    vg = (gc * lax.rsqrt(var + EPS) * lng_ref[...] + lnb_ref[...]).astype(BF16)
    n_chunks = rows // GM_CHUNK
    for g in range(GM_GROUPS):
        gl = slice(g * LANES, (g + 1) * LANES)
        rhs = jnp.concatenate([vg[c * GM_CHUNK:(c + 1) * GM_CHUNK, gl] for c in range(n_chunks)], axis=1)
        mixed = _dot(ws_ref[g], rhs)
        for c in range(n_chunks):
            rs = slice(c * GM_CHUNK, (c + 1) * GM_CHUNK)
            oc_o[rs, gl] = (gu[rs, gl] * (mixed[:, c * LANES:(c + 1) * LANES] + bs_ref[:, gl])).astype(BF16)


def _token_stage(latent, x, n_tiles, mod_l, mod_row, wl, consts, tables):
    t = TOKEN_TILE
    d = D_MODEL
    n = n_tiles * t
    full = lambda a: pl.BlockSpec(a.shape, lambda i: (0,) * a.ndim)
    weights = [wl["g_mix"], wl["w_a"], wl["gqa"], wl["gka"], wl["gcq"], wl["w_uq"], wl["gq"], wl["gckv"],
               wl["w_kn"], wl["w_vp"] if latent else wl["w_v"], wl["gk1"], wl["gk2"], wl["ln_g"], wl["ln_b"], wl["w_s"], wl["b_s"],
               consts["ones64"], consts["ones128"]]
    in_specs = [
        pl.BlockSpec((t, d), lambda i: (i, 0)),
        pl.BlockSpec((None, 1, 6 * d), lambda i: (mod_row(i), 0, 0)),
    ] + [full(a) for a in weights]
    args = [x, mod_l] + weights
    row = lambda w: pl.BlockSpec((t, w), lambda i: (i, 0))
    out_widths = [SWA_Q_COLS, 2 * LANES, LANES, MLA_PAD_COLS, MLA_PAD_COLS, MLA_HEADS * MLA_V, GM_WIDTH]
    out_specs = [row(w) for w in out_widths]
    out_shape = [jax.ShapeDtypeStruct((n, w), BF16) for w in out_widths]
    if latent:
        tiles_per_seq = tables["cos_a"].shape[0] // t
        tab = lambda: pl.BlockSpec((t, LANES), lambda i: (i % tiles_per_seq, 0))
        in_specs += [tab(), tab(), tab(), tab(), full(consts["rot_a"]), full(consts["rot_b"]),
                     full(consts["v_ones"])]
        args += [tables["cos_a"], tables["sin_a"], tables["cos_b"], tables["sin_b"],
                 consts["rot_a"], consts["rot_b"], consts["v_ones"]]
        out_specs[5] = pl.BlockSpec((MLA_VT_ROWS, t), lambda i: (0, i))
        out_shape[5] = jax.ShapeDtypeStruct((MLA_VT_ROWS, n), BF16)
        out_specs[2] = pl.BlockSpec((LANES, t), lambda i: (0, i))
        out_shape[2] = jax.ShapeDtypeStruct((LANES, n), BF16)
    else:
        cache_widths = [SWA_KV_COLS, SWA_KV_COLS, MLA_KV_LORA, MLA_ROPE]
        out_specs += [row(w) for w in cache_widths]
        out_shape += [jax.ShapeDtypeStruct((n, w), F32) for w in cache_widths]
    return pl.pallas_call(
        functools.partial(_token_kernel, latent),
        grid=(n_tiles,),
        in_specs=in_specs,
        out_specs=out_specs,
        out_shape=out_shape,
        compiler_params=_cparams("parallel"),
        name="token_stage_latent" if latent else "token_stage_ctx",
    )(*args)


def _softmax_pv(scores, values, extra=None):
    m = jnp.max(scores[0], axis=-1, keepdims=True)
    for s in scores[1:]:
        m = jnp.maximum(m, jnp.max(s, axis=-1, keepdims=True))
    if extra is not None:
        m = jnp.maximum(m, extra)
    den = jnp.exp(extra - m) if extra is not None else 0.0
    acc = None
    for s, v in zip(scores, values):
        p = jnp.exp(s - m)
        den = den + jnp.sum(p, axis=-1, keepdims=True)
        pv = _dot(p.astype(BF16), v)
        acc = pv if acc is None else acc + pv
    return acc / den


def _ctx_attn_kernel(sink_ref, qa_ref, kd_ref, va_ref, qb_ref, kb_ref, vb_ref, oa_ref, ob_ref):
    rows = qa_ref.shape[0]
    low = lax.broadcasted_iota(jnp.int32, (rows, LANES), 1) < SWA_HEAD_DIM
    va = va_ref[...]
    res = []
    for hd in range(SWA_HEADS):
        qblk = qa_ref[:, (hd // 2) * LANES:(hd // 2 + 1) * LANES]
        keep = low if hd % 2 == 0 else jnp.logical_not(low)
        qm = jnp.where(keep, qblk, jnp.zeros_like(qblk))
        kvh = hd // SWA_GROUP
        s = _nt_dot(qm, kd_ref[:, kvh * LANES:(kvh + 1) * LANES]) * (SWA_HEAD_DIM ** -0.5)
        res.append(_softmax_pv([s], [va], sink_ref[hd]))
    for j in range(SWA_GROUP):
        oa_ref[:, j * LANES:(j + 1) * LANES] = jnp.where(low, res[j], res[SWA_GROUP + j]).astype(BF16)
    res = []
    for h in range(MLA_HEADS):
        sl = slice(h * HEAD_PAD, (h + 1) * HEAD_PAD)
        s = _nt_dot(qb_ref[:, sl], kb_ref[:, sl]) * (MLA_QK ** -0.5)
        res.append(_softmax_pv([s], [vb_ref[:, (h // 2) * LANES:(h // 2 + 1) * LANES]]))
    for j in range(MLA_HEADS // 2):
        ob_ref[:, j * LANES:(j + 1) * LANES] = jnp.where(low, res[2 * j], res[2 * j + 1]).astype(BF16)


def _ctx_attention(sink, qa, kd, va, qb, kb, vb, n_seq, seq):
    row = lambda w: pl.BlockSpec((seq, w), lambda b: (b, 0))
    widths = [SWA_Q_COLS, 2 * LANES, LANES, MLA_PAD_COLS, MLA_PAD_COLS, MLA_HEADS * MLA_V]
    return pl.pallas_call(
        _ctx_attn_kernel,
        grid=(n_seq,),
        in_specs=[pl.BlockSpec(memory_space=pltpu.SMEM)] + [row(w) for w in widths],
        out_specs=[row(SWA_Q_COLS), row(MLA_HEADS * MLA_V)],
        out_shape=[jax.ShapeDtypeStruct((n_seq * seq, SWA_Q_COLS), BF16),
                   jax.ShapeDtypeStruct((n_seq * seq, MLA_HEADS * MLA_V), BF16)],
        compiler_params=_cparams("parallel"),
        name="ctx_attention",
    )(sink, qa, kd, va, qb, kb, vb)


def _window_attn_kernel(sink_ref, q_ref, kp_ref, kc_ref, kn_ref, vtp_ref, vtc_ref, vtn_ref, kctx_ref,
                        vtctx_ref, o_ref):
    w = SWA_WINDOW
    i = pl.program_id(1)
    nb = pl.num_programs(1)
    cols = SWA_GROUP * w
    c = (SWA_HEAD_DIM ** -0.5) * LOG2_E
    low = lax.broadcasted_iota(jnp.int32, (w, LANES), 1) < SWA_HEAD_DIM
    m_io = lax.broadcasted_iota(jnp.int32, (3 * w, cols), 0)
    rel = m_io - (lax.broadcasted_iota(jnp.int32, (3 * w, cols), 1) & (w - 1))
    first_row = jnp.where(i == 0, w, 0)
    end_row = jnp.where(i == nb - 1, 2 * w, 3 * w)
    valid = (rel >= 0) & (rel <= 2 * w) & (m_io >= first_row) & (m_io < end_row)
    head_of_col = lax.broadcasted_iota(jnp.int32, (1, cols), 1) >> (w.bit_length() - 1)
    vt_loc = jnp.concatenate([vtp_ref[...], vtc_ref[...], vtn_ref[...]], axis=1)
    vt_ctx = vtctx_ref[...]
    top_loc = lax.broadcasted_iota(jnp.int32, vt_loc.shape, 0) < SWA_HEAD_DIM
    top_ctx = lax.broadcasted_iota(jnp.int32, vt_ctx.shape, 0) < SWA_HEAD_DIM
    staged = []
    for kvh in range(SWA_KV_HEADS):
        qs = []
        sink = jnp.zeros((1, cols), F32)
        for g in range(SWA_GROUP):
            hd = kvh * SWA_GROUP + g
            qblk = q_ref[:, (hd // 2) * LANES:(hd // 2 + 1) * LANES]
            keep = low if hd % 2 == 0 else jnp.logical_not(low)
            qs.append(jnp.where(keep, qblk, jnp.zeros_like(qblk)))
            sink = jnp.where(head_of_col == g, sink_ref[hd], sink)
        qs = jnp.concatenate(qs, axis=0)
        ks = slice(kvh * LANES, (kvh + 1) * LANES)
        k_loc = jnp.concatenate([kp_ref[:, ks], kc_ref[:, ks], kn_ref[:, ks]], axis=0)
        st_ctx = _nt_dot(kctx_ref[:, ks], qs)
        st_loc = jnp.where(valid, _nt_dot(k_loc, qs), NEG_INF)
        staged.append((st_ctx, st_loc, sink))
    res = []
    for kvh, (st_ctx, st_loc, sink) in enumerate(staged):
        top = jnp.maximum(jnp.max(st_ctx, axis=0, keepdims=True), jnp.max(st_loc, axis=0, keepdims=True))
        m2 = jnp.maximum(top * (SWA_HEAD_DIM ** -0.5), sink) * LOG2_E
        p_ctx = jnp.exp2(st_ctx * c - m2).astype(BF16)
        p_loc = jnp.exp2(st_loc * c - m2).astype(BF16)
        own_loc = top_loc if kvh == 0 else jnp.logical_not(top_loc)
        own_ctx = top_ctx if kvh == 0 else jnp.logical_not(top_ctx)
        acc = (_dot(jnp.where(own_ctx, vt_ctx, jnp.ones_like(vt_ctx)), p_ctx)
               + _dot(jnp.where(own_loc, vt_loc, jnp.ones_like(vt_loc)), p_loc))
        own = slice(kvh * SWA_HEAD_DIM, (kvh + 1) * SWA_HEAD_DIM)
        other = (1 - kvh) * SWA_HEAD_DIM
        den = acc[other:other + 1] + jnp.exp2(sink * LOG2_E - m2)
        res.append(acc[own] / den)
    for j in range(SWA_GROUP):
        cs = slice(j * w, (j + 1) * w)
        o_ref[:, j * LANES:(j + 1) * LANES] = jnp.concatenate([res[0][:, cs], res[1][:, cs]], axis=0).T.astype(BF16)


def _window_attention(sink, qa, kd, vat, kd_ctx, vt_ctx, n_seq, seq):
    w = SWA_WINDOW
    nb = seq // w
    past = kd_ctx.shape[1]
    cur = lambda b, i: (b * nb + i, 0)
    prev = lambda b, i: (b * nb + jnp.maximum(i - 1, 0), 0)
    nxt = lambda b, i: (b * nb + jnp.minimum(i + 1, nb - 1), 0)
    flip = lambda f: (lambda b, i: f(b, i)[::-1])
    return pl.pallas_call(
        _window_attn_kernel,
        grid=(n_seq, nb),
        in_specs=[
            pl.BlockSpec(memory_space=pltpu.SMEM),
            pl.BlockSpec((w, SWA_Q_COLS), cur),
            pl.BlockSpec((w, 2 * LANES), prev),
            pl.BlockSpec((w, 2 * LANES), cur),
            pl.BlockSpec((w, 2 * LANES), nxt),
            pl.BlockSpec((LANES, w), flip(prev)),
            pl.BlockSpec((LANES, w), flip(cur)),
            pl.BlockSpec((LANES, w), flip(nxt)),
            pl.BlockSpec((None, past, 2 * LANES), lambda b, i: (b, 0, 0)),
            pl.BlockSpec((None, LANES, past), lambda b, i: (b, 0, 0)),
        ],
        out_specs=pl.BlockSpec((w, SWA_Q_COLS), cur),
        out_shape=jax.ShapeDtypeStruct((n_seq * seq, SWA_Q_COLS), BF16),
        compiler_params=_cparams("parallel", "parallel"),
        name="window_attention",
    )(sink, qa, kd, kd, kd, vat, vat, vat, kd_ctx, vt_ctx)


def _latent_mla_kernel(bound_ref, q_ref, k_ref, vt_ref, kctx_ref, vtctx_ref, o_ref):
    kc = MLA_KEY_CHUNK
    c = (MLA_QK ** -0.5) * LOG2_E
    chunks = ([(kctx_ref, vtctx_ref, j) for j in range(kctx_ref.shape[0] // kc)]
              + [(k_ref, vt_ref, j) for j in range(k_ref.shape[0] // kc)])
    heads = [slice(e * HEAD_PAD, (e + 1) * HEAD_PAD) for e in range(2)]

    def scores(n):
        keys_ref, _, j = chunks[n]
        return [_nt_dot(keys_ref[j * kc:(j + 1) * kc, sl], q_ref[:, sl]) for sl in heads]

    def attend(update):
        state = [None, None]
        pending = [scores(n) for n in range(min(MLA_SCORE_LOOKAHEAD, len(chunks)))]
        for n, (_, vals_ref, j) in enumerate(chunks):
            if n + MLA_SCORE_LOOKAHEAD < len(chunks):
                pending.append(scores(n + MLA_SCORE_LOOKAHEAD))
            st = pending.pop(0)
            for e in range(2):
                vals = vals_ref[e * MLA_V_ROWS:(e + 1) * MLA_V_ROWS, j * kc:(j + 1) * kc]
                state[e] = update(state[e], st[e], vals)
        return state

    def write(accs):
        outs = [a[0:MLA_V] / a[MLA_V:MLA_V + 1] for a in accs]
        o_ref[...] = jnp.concatenate(outs, axis=0).T.astype(BF16)

    bound = bound_ref[0]

    def bounded(acc, st, vals):
        pv = _dot(vals, jnp.exp2(st * c - bound).astype(BF16))
        return pv if acc is None else acc + pv

    accs = attend(bounded)
    smallest = jnp.minimum(jnp.min(accs[0][MLA_V:MLA_V + 1]), jnp.min(accs[1][MLA_V:MLA_V + 1]))
    safe = smallest >= MLA_MIN_DENOMINATOR

    @pl.when(safe)
    def _():
        write(accs)

    @pl.when(jnp.logical_not(safe))
    def _():
        def online(state, st, vals):
            cmax = jnp.max(st, axis=0, keepdims=True)
            if state is None:
                return cmax, _dot(vals, jnp.exp2((st - cmax) * c).astype(BF16))
            m, acc = state
            m_new = jnp.maximum(m, cmax)
            p = jnp.exp2((st - m_new) * c).astype(BF16)
            return m_new, acc * jnp.exp2((m - m_new) * c) + _dot(vals, p)

        write([acc for _, acc in attend(online)])


def _latent_mla(bound, qb, kb, vb, k_ctx, v_ctx, n_seq, seq):
    tq = MLA_Q_TILE
    nq = seq // tq
    past = k_ctx.shape[1]
    pair = 2 * HEAD_PAD
    assert past % MLA_KEY_CHUNK == 0 and seq % MLA_KEY_CHUNK == 0
    return pl.pallas_call(
        _latent_mla_kernel,
        grid=(n_seq, MLA_HEADS // 2, nq),
        in_specs=[
            pl.BlockSpec(memory_space=pltpu.SMEM),
            pl.BlockSpec((tq, pair), lambda b, p, i: (b * nq + i, p)),
            pl.BlockSpec((seq, pair), lambda b, p, i: (b, p)),
            pl.BlockSpec((2 * MLA_V_ROWS, seq), lambda b, p, i: (p, b)),
            pl.BlockSpec((None, past, pair), lambda b, p, i: (b, 0, p)),
            pl.BlockSpec((None, 2 * MLA_V_ROWS, past), lambda b, p, i: (b, p, 0)),
        ],
        out_specs=pl.BlockSpec((tq, LANES), lambda b, p, i: (b * nq + i, p)),
        out_shape=jax.ShapeDtypeStruct((n_seq * seq, MLA_HEADS * MLA_V), BF16),
        compiler_params=_cparams("parallel", "parallel", "parallel"),
        name="latent_mla",
    )(bound, qb, kb, vb, k_ctx, v_ctx)


def _route(logits):
    lane = lax.broadcasted_iota(jnp.int32, logits.shape, 1).astype(F32)
    big = float(4 * LANES)
    gmask = lane < N_EXPERT_GROUPS
    gmax = jnp.max(jnp.where(gmask, logits, -jnp.inf), axis=-1, keepdims=True)
    gsum = jnp.sum(jnp.where(gmask, jnp.exp(logits - gmax), 0.0), axis=-1, keepdims=True)
    g_w = 1.0 / gsum
    g_idx = jnp.min(jnp.where(gmask & (logits == gmax), lane, big), axis=-1, keepdims=True)
    first = ROUTER_LANE0 + EXPERTS_PER_GROUP * g_idx
    emask = (lane >= first) & (lane < first + EXPERTS_PER_GROUP)
    el = jnp.where(emask, logits, -jnp.inf)
    m1 = jnp.max(el, axis=-1, keepdims=True)
    i1 = jnp.min(jnp.where(emask & (el == m1), lane, big), axis=-1, keepdims=True)
    el2 = jnp.where(lane == i1, -jnp.inf, el)
    m2 = jnp.max(el2, axis=-1, keepdims=True)
    i2 = jnp.min(jnp.where(emask & (el2 == m2), lane, big), axis=-1, keepdims=True)
    esum = jnp.sum(jnp.where(emask, jnp.exp(logits - m1), 0.0), axis=-1, keepdims=True)
    p1 = 1.0 / esum
    p2 = jnp.exp(m2 - m1) / esum
    tot = p1 + p2
    combine = jnp.where(lane == i1, g_w * (p1 / tot), 0.0) + jnp.where(lane == i2, g_w * (p2 / tot), 0.0)
    return jnp.where(lane == g_idx, 1.0, combine)


def _finish_kernel(x_ref, oa_ref, ob_ref, oc_ref, mod_ref, gmix_ref, gffn_ref, wg_ref, woa_ref, wob_ref,
                   woc_ref, wout_ref, wr_ref, br_ref, x1_ref, h2_ref, comb_ref):
    d = D_MODEL
    x = x_ref[...]
    h = _modulated_norm(x, gmix_ref[...], mod_ref[:, 0:d], mod_ref[:, d:2 * d]).astype(BF16)
    gates = _dot(h, wg_ref[...])
    merged = (jax.nn.sigmoid(gates[:, 0:d]) * _dot(oa_ref[...], woa_ref[...])
              + jax.nn.sigmoid(gates[:, d:2 * d]) * _dot(ob_ref[...], wob_ref[...])
              + jax.nn.sigmoid(gates[:, 2 * d:3 * d]) * _dot(oc_ref[...], woc_ref[...]))
    x1 = x + mod_ref[:, 2 * d:3 * d] * _dot(merged.astype(BF16), wout_ref[...])
    x1_ref[...] = x1
    h2 = _modulated_norm(x1, gffn_ref[...], mod_ref[:, 3 * d:4 * d], mod_ref[:, 4 * d:5 * d])
    h2_hi = h2.astype(BF16)
    h2_ref[...] = h2_hi
    comb_ref[...] = _route(_dot(h2_hi, wr_ref[...]) + br_ref[...])


def _finish(x_all, oa, ob, oc, mod_l, mod_row, wl):
    t = TOKEN_TILE
    d = D_MODEL
    n = x_all.shape[0]
    full = lambda a: pl.BlockSpec(a.shape, lambda i: (0,) * a.ndim)
    row = lambda w: pl.BlockSpec((t, w), lambda i: (i, 0))
    weights = [wl["g_mix"], wl["g_ffn"], wl["w_g"], wl["w_o_a"], wl["w_o_b"], wl["w_o_c"], wl["w_out"],
               wl["w_r"], wl["b_r"]]
    return pl.pallas_call(
        _finish_kernel,
        grid=(n // t,),
        in_specs=[row(d), row(SWA_Q_COLS), row(MLA_HEADS * MLA_V), row(GM_WIDTH),
                  pl.BlockSpec((None, 1, 6 * d), lambda i: (mod_row(i), 0, 0))] + [full(a) for a in weights],
        out_specs=[row(d), row(d), row(LANES)],
        out_shape=[jax.ShapeDtypeStruct((n, d), F32), jax.ShapeDtypeStruct((n, d), BF16),
                   jax.ShapeDtypeStruct((n, LANES), F32)],
        compiler_params=_cparams("parallel"),
        name="finish",
    )(x_all, oa, ob, oc, mod_l, *weights)


def _moe_kernel(h_ref, route_ref, x1_ref, mod_ref, tri_ref, wg_ref, wu_ref, wd_ref, o_ref,
                xs, rs, ys, dest_cols, windows):
    d = D_MODEL
    t = h_ref.shape[0]
    ch = MOE_CHUNK
    blk = MXU_DIM
    s = pl.program_id(1)
    steps_per_group = EXPERTS_PER_GROUP // MOE_EXPERTS_PER_STEP
    g = s // steps_per_group
    j0 = (s % steps_per_group) * MOE_EXPERTS_PER_STEP

    @pl.when(s == 0)
    def _():
        route = route_ref[...]
        rt = route.T
        rank = _dot(rt.astype(BF16), tri_ref[...])
        rank_c = rank.T
        lane1 = lax.broadcasted_iota(jnp.int32, (1, LANES), 1)
        dest_row = jnp.zeros((1, t), F32)
        start_vec = jnp.zeros((1, LANES), F32)
        start = jnp.zeros((1, 1), F32)
        for gg in range(N_EXPERT_GROUPS):
            ind = rt[gg:gg + 1, :]
            count = jnp.sum(ind, axis=-1, keepdims=True)
            dest_row = dest_row + ind * (rank[gg:gg + 1, :] + start)
            start_vec = jnp.where(lane1 == gg, start, start_vec)
            first = jnp.sum(start).astype(jnp.int32)
            base = (first // BF16_SUBLANES) * BF16_SUBLANES
            windows[2 * gg] = base
            windows[2 * gg + 1] = (first - base + jnp.sum(count).astype(jnp.int32) + (ch - 1)) // ch
            start = start + count
        lane_t = lax.broadcasted_iota(jnp.int32, (t, LANES), 1)
        dest_cols[...] = jnp.sum(jnp.where(lane_t < N_EXPERT_GROUPS, route * (rank_c + start_vec), 0.0),
                                 axis=-1, keepdims=True)
        hi = route.astype(BF16)
        split = jnp.concatenate([hi, (route - hi.astype(F32)).astype(BF16)], axis=1)
        for b in range(t // blk):
            r_io = (lax.broadcasted_iota(jnp.int32, (blk, t), 0) + b * blk).astype(F32)
            perm = jnp.where(dest_row == r_io, 1.0, 0.0).astype(BF16)
            xs[b * blk:(b + 1) * blk, :] = _dot(perm, h_ref[...]).astype(BF16)
            r2 = _dot(perm, split)
            rs[b * blk:(b + 1) * blk, :] = r2[:, 0:LANES] + r2[:, LANES:2 * LANES]
        xs[t:, :] = jnp.zeros((xs.shape[0] - t, d), BF16)
        rs[t:, :] = jnp.zeros((rs.shape[0] - t, LANES), F32)
        ys[...] = jnp.zeros_like(ys)

    base = windows[2 * g]
    n_windows = windows[2 * g + 1]

    def ffn(w, carry):
        rows = pl.ds(pl.multiple_of(base + w * ch, BF16_SUBLANES), ch)
        x = xs[rows, :]
        wt = rs[rows, :]
        lane = lax.broadcasted_iota(jnp.int32, (ch, LANES), 1)
        total = None
        for k in range(MOE_EXPERTS_PER_STEP):
            a = _dot(x, wg_ref[k])
            hid = (a * jax.nn.sigmoid(a)) * _dot(x, wu_ref[k])
            y = _dot(hid.astype(BF16), wd_ref[k])
            w_col = jnp.sum(jnp.where(lane == ROUTER_LANE0 + EXPERTS_PER_GROUP * g + j0 + k, wt, 0.0), axis=-1,
                            keepdims=True)
            total = w_col * y if total is None else total + w_col * y
        ys[rows, :] += total
        return carry

    lax.fori_loop(0, n_windows, ffn, 0)

    @pl.when(s == pl.num_programs(1) - 1)
    def _():
        dest_col = dest_cols[...]
        for b in range(t // blk):
            c_io = (lax.broadcasted_iota(jnp.int32, (t, blk), 1) + b * blk).astype(F32)
            perm_t = jnp.where(dest_col == c_io, 1.0, 0.0).astype(BF16)
            part = _dot(perm_t, ys[b * blk:(b + 1) * blk, :].astype(BF16))
            if b == 0:
                o_ref[...] = part
            else:
                o_ref[...] += part
        o_ref[...] = x1_ref[...] + mod_ref[:, 5 * d:6 * d] * o_ref[...]


def _moe(h2, route, x1, mod_l, mod_row, wl, tri):
    d = D_MODEL
    n = h2.shape[0]
    tile = tri.shape[0]
    sorted_rows = tile + MOE_CHUNK
    eps = MOE_EXPERTS_PER_STEP
    row = lambda w: pl.BlockSpec((tile, w), lambda i, e: (i, 0))
    return pl.pallas_call(
        _moe_kernel,
        grid=(n // tile, N_EXPERTS // eps),
        in_specs=[row(d), row(LANES), row(d),
                  pl.BlockSpec((None, 1, 6 * d), lambda i, e: (mod_row(i), 0, 0)),
                  pl.BlockSpec((tile, tile), lambda i, e: (0, 0)),
                  pl.BlockSpec((eps, d, EXPERT_FF), lambda i, e: (e, 0, 0)),
                  pl.BlockSpec((eps, d, EXPERT_FF), lambda i, e: (e, 0, 0)),
                  pl.BlockSpec((eps, EXPERT_FF, d), lambda i, e: (e, 0, 0))],
        out_specs=row(d),
        out_shape=jax.ShapeDtypeStruct((n, d), F32),
        scratch_shapes=[pltpu.VMEM((sorted_rows, d), BF16), pltpu.VMEM((sorted_rows, LANES), F32),
                        pltpu.VMEM((sorted_rows, d), F32), pltpu.VMEM((tile, 1), F32),
                        pltpu.SMEM((2 * N_EXPERT_GROUPS,), jnp.int32)],
        compiler_params=_cparams("parallel", "arbitrary"),
        name="moe_grouped",
    )(h2, route, x1, mod_l, tri, wl["w_gate"], wl["w_up"], wl["w_down"])


def _block_ones(group):
    idx = np.arange(MXU_DIM)
    return jnp.asarray((idx[:, None] // group) == (idx[None, :] // group), BF16)


def _rotate_half_matrix(block, rot_start, rot_dim):
    half = rot_dim // 4
    r = np.zeros((MXU_DIM, MXU_DIM), np.float32)
    for j in range(MXU_DIM):
        o = j % block - rot_start
        if 0 <= o < rot_dim:
            partner = j + half if (o % (2 * half)) < half else j - half
            r[partner, j] = 1.0
    return jnp.asarray(r, BF16)


def _rope_tables(n_tokens, block, rot_start, rot_dim):
    half = rot_dim // 4
    tok = np.arange(n_tokens)
    pos = np.stack([tok // GRID_W, tok % GRID_W], axis=1).astype(np.float32)
    inv_freq = (ROPE_THETA ** (-np.arange(half, dtype=np.float32) / half)).astype(np.float32)
    lane = np.arange(LANES)
    o = lane % block - rot_start
    rot = (o >= 0) & (o < rot_dim)
    oc = np.where(rot, o, 0)
    axis = oc // (2 * half)
    freq = inv_freq[oc % half]
    sign = np.where((oc % (2 * half)) < half, -1.0, 1.0).astype(np.float32)
    ang = jnp.asarray(pos[:, axis]) * jnp.asarray(freq)[None, :]
    rot_j = jnp.asarray(rot)[None, :]
    cos = jnp.where(rot_j, jnp.cos(ang), 1.0)
    sin = jnp.where(rot_j, jnp.sin(ang) * jnp.asarray(sign)[None, :], 0.0)
    return cos.astype(F32), sin.astype(F32)


def _pad_heads(w, heads, width, padded=HEAD_PAD):
    lead = w.shape[:-1]
    w = w.reshape(lead + (heads, width))
    w = jnp.pad(w, [(0, 0)] * len(lead) + [(0, 0), (0, padded - width)])
    return w.reshape(lead + (heads * padded,))


def _prep_weights(p):
    depth = p["w_in"].shape[0]
    w_in = p["w_in"]
    c = np.cumsum((0, SWA_Q_COLS, SWA_KV_COLS, SWA_KV_COLS, MLA_Q_LORA, MLA_KV_LORA, MLA_ROPE, GM_WIDTH,
                   GM_WIDTH))
    kr = w_in[..., c[5]:c[6]]
    zero = jnp.zeros_like(kr)
    w_a = jnp.concatenate([w_in[..., c[0]:c[5]], kr, zero, kr, zero, w_in[..., c[6]:c[8]]], axis=-1)
    ukv = p["mla_w_ukv"].reshape(depth, MLA_KV_LORA, MLA_HEADS, MLA_NOPE + MLA_V)
    order = np.array([0, 4, 1, 5, 2, 6, 3, 7])
    w_o_a = p["w_o_a"].reshape(depth, SWA_HEADS, SWA_HEAD_DIM, D_MODEL)[:, order].reshape(depth, SWA_Q_COLS,
                                                                                          D_MODEL)
    w_r = jnp.concatenate([p["w_rg"], p["w_re"]], axis=-1)
    w_r = jnp.pad(w_r, ((0, 0), (0, 0), (0, LANES - w_r.shape[-1])))
    b_r = jnp.concatenate([p["b_rg"], p["b_re"]], axis=-1)
    b_r = jnp.pad(b_r, ((0, 0), (0, LANES - b_r.shape[-1])))
    k_gain = p["mla_k_norm"]
    row = lambda a: a.reshape(depth, 1, a.shape[-1]).astype(F32)
    bs_full = jnp.repeat(jnp.swapaxes(p["gm_b_s"], 1, 2), LANES, axis=-1)
    return {
        "g_mix": row(p["g_mix"]), "g_ffn": row(p["g_ffn"]),
        "w_a": w_a.astype(BF16), "w_g": w_in[..., _GATE0:].astype(BF16),
        "gqa": row(jnp.tile(p["swa_q_norm"], (1, SWA_HEADS))),
        "gka": row(jnp.tile(p["swa_k_norm"], (1, SWA_KV_HEADS))),
        "gcq": row(p["mla_cq_norm"]), "gckv": row(p["mla_ckv_norm"]),
        "w_uq": _pad_heads(p["mla_w_uq"], MLA_HEADS, MLA_QK).astype(BF16),
        "gq": row(_pad_heads(jnp.tile(p["mla_q_norm"], (1, MLA_HEADS)), MLA_HEADS, MLA_QK)),
        "w_kn": _pad_heads(ukv[..., :MLA_NOPE].reshape(depth, MLA_KV_LORA, -1), MLA_HEADS, MLA_NOPE).astype(BF16),
        "w_v": ukv[..., MLA_NOPE:].reshape(depth, MLA_KV_LORA, -1).astype(BF16),
        "w_vp": _pad_heads(ukv[..., MLA_NOPE:].reshape(depth, MLA_KV_LORA, -1), MLA_HEADS, MLA_V,
                           MLA_V_ROWS).astype(BF16),
        "gk1": row(_pad_heads(jnp.tile(k_gain[:, :MLA_NOPE], (1, MLA_HEADS)), MLA_HEADS, MLA_NOPE)),
        "gk2": row(jnp.pad(k_gain[:, MLA_NOPE:], ((0, 0), (MLA_NOPE, LANES - MLA_QK)))),
        "ln_g": row(p["gm_ln_g"]), "ln_b": row(p["gm_ln_b"]),
        "w_s": p["gm_w_s"].astype(BF16), "b_s": bs_full.astype(F32),
        "w_o_a": w_o_a.astype(BF16), "w_o_b": p["w_o_b"].astype(BF16), "w_o_c": p["w_o_c"].astype(BF16),
        "w_out": p["w_out"].astype(BF16),
        "w_r": w_r.astype(BF16), "b_r": row(b_r),
        "w_gate": p["w_gate"].astype(BF16), "w_up": p["w_up"].astype(BF16), "w_down": p["w_down"].astype(BF16),
    }


def kernel(x_prompt, x_sample, cache_swa_k, cache_swa_v, cache_mla_ckv, cache_mla_krope, c, c_ctx, w_mod, b_mod, g_mix, g_ffn, w_in, swa_q_norm, swa_k_norm, swa_sink, mla_cq_norm, mla_ckv_norm, mla_w_uq, mla_w_ukv, mla_q_norm, mla_k_norm, gm_ln_g, gm_ln_b, gm_w_s, gm_b_s, w_o_a, w_o_b, w_o_c, w_out, w_rg, b_rg, w_re, b_re, w_gate, w_up, w_down):
    d = D_MODEL
    n_ctx_seq, ctx_len, _ = x_prompt.shape
    n_lat_seq, lat_len, _ = x_sample.shape
    depth = w_in.shape[0]
    past = cache_swa_k.shape[2]
    n_ctx = n_ctx_seq * ctx_len
    n_lat = n_lat_seq * lat_len
    t = TOKEN_TILE
    assert n_ctx % t == 0 and ctx_len % GM_CHUNK == 0 and lat_len % t == 0 and lat_len % MLA_Q_TILE == 0 and lat_len % GRID_W == 0 and n_lat_seq < 8
    ctx_tiles = n_ctx // t
    lat_tiles = n_lat // t
    moe_tile = next(m for m in MOE_TILE_CANDIDATES if n_ctx % m == 0 and lat_len % m == 0)

    params = dict(w_in=w_in, g_mix=g_mix, g_ffn=g_ffn, swa_q_norm=swa_q_norm, swa_k_norm=swa_k_norm,
                  mla_cq_norm=mla_cq_norm, mla_ckv_norm=mla_ckv_norm, mla_w_uq=mla_w_uq, mla_w_ukv=mla_w_ukv,
                  mla_q_norm=mla_q_norm, mla_k_norm=mla_k_norm, gm_ln_g=gm_ln_g, gm_ln_b=gm_ln_b,
                  gm_w_s=gm_w_s, gm_b_s=gm_b_s, w_o_a=w_o_a, w_o_b=w_o_b, w_o_c=w_o_c, w_out=w_out,
                  w_rg=w_rg, b_rg=b_rg, w_re=w_re, b_re=b_re, w_gate=w_gate, w_up=w_up, w_down=w_down)
    w_all = _prep_weights(params)
    consts = {
        "ones64": _block_ones(SWA_HEAD_DIM), "ones128": _block_ones(HEAD_PAD),
        "rot_a": _rotate_half_matrix(SWA_HEAD_DIM, 0, SWA_HEAD_DIM),
        "rot_b": _rotate_half_matrix(HEAD_PAD, MLA_NOPE, MLA_ROPE),
        "v_ones": jnp.asarray((np.arange(MLA_VT_ROWS) % MLA_V_ROWS >= MLA_V).astype(np.float32)[None, :]),
        "tri": jnp.asarray(np.triu(np.ones((moe_tile, moe_tile), np.float32), k=1), BF16),
    }
    cos_a, sin_a = _rope_tables(lat_len, SWA_HEAD_DIM, 0, SWA_HEAD_DIM)
    cos_b, sin_b = _rope_tables(lat_len, HEAD_PAD, MLA_NOPE, MLA_ROPE)
    tables = {"cos_a": cos_a, "sin_a": sin_a, "cos_b": cos_b, "sin_b": sin_b}

    cond8 = jnp.zeros((8, d), F32).at[:n_lat_seq].set(c).at[n_lat_seq].set(c_ctx)
    mods = _adaln(cond8, w_mod, b_mod).reshape(depth, 8, 1, 6 * d)

    k0 = cache_swa_k[:, :, :, 0, :]
    k1 = cache_swa_k[:, :, :, 1, :]
    kd_ctx_all = jnp.concatenate([k0, k0, k1, k1], axis=-1).astype(BF16)
    vt_swa_ctx_all = jnp.swapaxes(cache_swa_v.reshape(n_lat_seq, depth, past, SWA_KV_COLS), 2, 3).astype(BF16)
    kr_hi = jnp.pad(cache_mla_krope, ((0, 0), (0, 0), (0, 0), (MLA_NOPE, LANES - MLA_QK)))
    kb_ctx_all, vt_ctx_all = _ctx_keys(cache_mla_ckv, kr_hi, w_all["w_kn"], w_all["w_vp"], w_all["gk1"],
                                       w_all["gk2"], consts["v_ones"], consts["ones128"])

    ctx_row = lambda i: n_lat_seq
    lat_row = lambda i: i // (lat_len // t)
    lat_moe_row = lambda i: i // (lat_len // moe_tile)

    x_c = x_prompt.reshape(n_ctx, d)
    x_s = x_sample.reshape(n_lat, d)
    caches = [[], [], [], []]
    for l in range(depth):
        wl = {k: v[l] for k, v in w_all.items()}
        mod_l = mods[l]
        sink = swa_sink[l].reshape(SWA_HEADS)
        (qa_c, kd_c, va_c, qb_c, kb_c, vb_c, oc_c, ka_f, va_f, ckv_f, kr_f) = _token_stage(
            False, x_c, ctx_tiles, mod_l, ctx_row, wl, consts, None)
        (qa_s, kd_s, va_s, qb_s, kb_s, vb_s, oc_s) = _token_stage(
            True, x_s, lat_tiles, mod_l, lat_row, wl, consts, tables)
        for dst, val in zip(caches, (ka_f, va_f, ckv_f, kr_f)):
            dst.append(val)
        oa_c, ob_c = _ctx_attention(sink, qa_c, kd_c, va_c, qb_c, kb_c, vb_c, n_ctx_seq, ctx_len)
        oa_s = _window_attention(sink, qa_s, kd_s, va_s, kd_ctx_all[:, l], vt_swa_ctx_all[:, l], n_lat_seq,
                                 lat_len)
        mla_bound = (MLA_QK * jnp.max(jnp.abs(mla_q_norm[l])) * jnp.max(jnp.abs(mla_k_norm[l]))
                     * (MLA_QK ** -0.5) * LOG2_E).reshape(1).astype(F32)
        ob_s = _latent_mla(mla_bound, qb_s, kb_s, vb_s, kb_ctx_all[l], vt_ctx_all[l], n_lat_seq, lat_len)
        x1_c, h2_c, route_c = _finish(x_c, oa_c, ob_c, oc_c, mod_l, ctx_row, wl)
        x1_s, h2_s, route_s = _finish(x_s, oa_s, ob_s, oc_s, mod_l, lat_row, wl)
        x_c = _moe(h2_c, route_c, x1_c, mod_l, ctx_row, wl, consts["tri"])
        x_s = _moe(h2_s, route_s, x1_s, mod_l, lat_moe_row, wl, consts["tri"])

    y_prompt = x_c.reshape(n_ctx_seq, ctx_len, d)
    y_sample = x_s.reshape(n_lat_seq, lat_len, d)
    stack = lambda vals, tail: jnp.stack([v.reshape((n_ctx_seq, ctx_len) + tail) for v in vals], axis=1)
    return (y_prompt, y_sample,
            stack(caches[0], (SWA_KV_HEADS, SWA_HEAD_DIM)), stack(caches[1], (SWA_KV_HEADS, SWA_HEAD_DIM)),
            stack(caches[2], (MLA_KV_LORA,)), stack(caches[3], (MLA_ROPE,)))
```
